```python
import math
import jax, jax.numpy as jnp
from jax import lax
import numpy as np

D_MODEL = 1024
BATCH = 1
SEQ = 16384
DEPTH = 2
DEC_BATCH = 16
DEC_SEQ = 32
PAST_LEN = 1024

CHUNK = 64
N_MIXERS = 2
N_ATTN_LAYERS = (DEPTH + 1) // 2
N_SSM_LAYERS = DEPTH // 2
N_HEADS = 8
HEAD_DIM = 64
V_DIM = 2 * HEAD_DIM
ROT_DIM = HEAD_DIM // 4
ROPE_THETA = 500000.0
Q_BLOCK = 128
SSM_GROUP = 16
N_SSM_GROUPS = D_MODEL // SSM_GROUP
SSM_STATE = 64
N_EXPERT_GROUPS = 4
EXPERTS_PER_GROUP = 4
N_EXPERTS = N_EXPERT_GROUPS * EXPERTS_PER_GROUP
TOP_K_IN_GROUP = 2
D_EXPERT = D_MODEL // 4
RMS_EPS = 1e-6

kernel_name = 'chunk_causal_diffattn_s5_hmoe_step'


def rmsnorm(x, w):
    xf = x.astype(jnp.float32)
    xf = xf * lax.rsqrt(jnp.mean(xf * xf, axis=-1, keepdims=True) + RMS_EPS)
    return (xf * w.astype(jnp.float32)).astype(x.dtype)


def lambda_init_for(layer_idx):
    return 0.8 - 0.6 * math.exp(-0.3 * layer_idx)


def partial_rope(x, pos):
    half = ROT_DIM // 2
    inv_freq = ROPE_THETA ** (-jnp.arange(half, dtype=jnp.float32) * 2.0 / ROT_DIM)
    ang = pos.astype(jnp.float32)[:, None] * inv_freq[None, :]
    cos = jnp.cos(ang)[:, None, None, :].astype(x.dtype)
    sin = jnp.sin(ang)[:, None, None, :].astype(x.dtype)
    x1, x2, rest = x[..., :half], x[..., half:ROT_DIM], x[..., ROT_DIM:]
    return jnp.concatenate([x1 * cos - x2 * sin, x2 * cos + x1 * sin, rest], axis=-1)


def diff_qkv(h, pos, w_qkv):
    b, l, _ = h.shape
    q, k, v = jnp.split(h @ w_qkv, 3, axis=-1)
    q = partial_rope(q.reshape(b, l, N_HEADS, 2, HEAD_DIM), pos)
    k = partial_rope(k.reshape(b, l, N_HEADS, 2, HEAD_DIM), pos)
    return q, k, v.reshape(b, l, N_HEADS, V_DIM)


def diff_lambda(lq1, lk1, lq2, lk2, lam_init):
    f = lambda a: a.astype(jnp.float32)
    return jnp.exp(jnp.sum(f(lq1) * f(lk1))) - jnp.exp(jnp.sum(f(lq2) * f(lk2))) + lam_init


def diff_attend(q, k, v, mask, lam):
    s = jnp.einsum('bqhcd,bkhcd->bchqk', q, k).astype(jnp.float32) * (HEAD_DIM ** -0.5)
    if mask is not None:
        s = jnp.where(mask, s, -jnp.inf)
    p = jax.nn.softmax(s, axis=-1)
    a = p[:, 0] - lam * p[:, 1]
    return jnp.einsum('bhqk,bkhe->bqhe', a.astype(v.dtype), v)


def diff_head_out(o, w_subln, lam_init, w_o):
    o = rmsnorm(o, w_subln) * (1.0 - lam_init)
    return o.reshape(o.shape[0], o.shape[1], D_MODEL) @ w_o


def diff_attn_prompt(h, w_qkv, lq1, lk1, lq2, lk2, w_subln, w_o, lam_init):
    b, l, _ = h.shape
    pos = jnp.arange(l, dtype=jnp.int32)
    q, k, v = diff_qkv(h, pos, w_qkv)
    lam = diff_lambda(lq1, lk1, lq2, lk2, lam_init)
    nblk = l // Q_BLOCK
    q_blocks = jnp.moveaxis(q.reshape(b, nblk, Q_BLOCK, N_HEADS, 2, HEAD_DIM), 1, 0)
    key_chunk = pos // CHUNK

    def one_block(args):
        qb, i = args
        q_chunk = (i * Q_BLOCK + jnp.arange(Q_BLOCK, dtype=jnp.int32)) // CHUNK
        mask = key_chunk[None, :] <= q_chunk[:, None]
        return diff_attend(qb, k, v, mask, lam)

    o = lax.map(one_block, (q_blocks, jnp.arange(nblk, dtype=jnp.int32)))
    o = jnp.moveaxis(o, 0, 1).reshape(b, l, N_HEADS, V_DIM)
    y = diff_head_out(o, w_subln, lam_init, w_o)
    return y, k.reshape(b, l, N_HEADS, 2 * HEAD_DIM), v


def diff_attn_sample(h, cache_k, cache_v, w_qkv, lq1, lk1, lq2, lk2, w_subln, w_o, lam_init):
    b, t, _ = h.shape
    pos = PAST_LEN + jnp.arange(t, dtype=jnp.int32)
    q, k, v = diff_qkv(h, pos, w_qkv)
    lam = diff_lambda(lq1, lk1, lq2, lk2, lam_init)
    past = cache_k.shape[1]
    k_all = jnp.concatenate([cache_k.reshape(b, past, N_HEADS, 2, HEAD_DIM), k], axis=1)
    v_all = jnp.concatenate([cache_v, v], axis=1)
    o = diff_attend(q, k_all, v_all, None, lam)
    y = diff_head_out(o, w_subln, lam_init, w_o)
    return y, k.reshape(b, t, N_HEADS, 2 * HEAD_DIM), v


def s5_discretize(lam_re, lam_im, log_dt, b_re, b_im):
    lam_re, lam_im = lam_re.astype(jnp.float32), lam_im.astype(jnp.float32)
    b_re, b_im = b_re.astype(jnp.float32), b_im.astype(jnp.float32)
    dt = jnp.exp(log_dt.astype(jnp.float32))[:, None]
    z_re, z_im = lam_re * dt, lam_im * dt
    mag = jnp.exp(z_re)
    lb_re, lb_im = mag * jnp.cos(z_im), mag * jnp.sin(z_im)
    n_re, n_im = lb_re - 1.0, lb_im
    den = lam_re * lam_re + lam_im * lam_im
    c_re = (n_re * lam_re + n_im * lam_im) / den
    c_im = (n_im * lam_re - n_re * lam_im) / den
    bb_re = c_re[..., None] * b_re - c_im[..., None] * b_im
    bb_im = c_re[..., None] * b_im + c_im[..., None] * b_re
    return lb_re, lb_im, bb_re, bb_im


def _affine_combine(e1, e2):
    a1r, a1i, b1r, b1i = e1
    a2r, a2i, b2r, b2i = e2
    return (a2r * a1r - a2i * a1i, a2r * a1i + a2i * a1r,
            a2r * b1r - a2i * b1i + b2r, a2r * b1i + a2i * b1r + b2i)


def s5_mixer(h, x0_re, x0_im, lam_re, lam_im, log_dt, b_re, b_im, c_re, c_im, d_skip, w_glu, w_gate):
    b, l, _ = h.shape
    u = h.astype(jnp.float32).reshape(b, l, N_SSM_GROUPS, SSM_GROUP)
    lb_re, lb_im, bb_re, bb_im = s5_discretize(lam_re, lam_im, log_dt, b_re, b_im)
    bu_re = jnp.einsum('blgc,gpc->blgp', u, bb_re)
    bu_im = jnp.einsum('blgc,gpc->blgp', u, bb_im)
    if x0_re is not None:
        x0r, x0i = x0_re.astype(jnp.float32), x0_im.astype(jnp.float32)
        bu_re = bu_re.at[:, 0].add(lb_re * x0r - lb_im * x0i)
        bu_im = bu_im.at[:, 0].add(lb_re * x0i + lb_im * x0r)
    a_re = jnp.broadcast_to(lb_re, bu_re.shape)
    a_im = jnp.broadcast_to(lb_im, bu_re.shape)
    _, _, s_re, s_im = lax.associative_scan(_affine_combine, (a_re, a_im, bu_re, bu_im), axis=1)
    y = (jnp.einsum('blgp,gcp->blgc', s_re, c_re.astype(jnp.float32))
         - jnp.einsum('blgp,gcp->blgc', s_im, c_im.astype(jnp.float32))
         + d_skip.astype(jnp.float32).reshape(N_SSM_GROUPS, SSM_GROUP) * u)
    z = jax.nn.gelu(y.reshape(b, l, D_MODEL).astype(h.dtype))
    out = (z @ w_glu) * jax.nn.sigmoid(z @ w_gate)
    return out, s_re[:, -1], s_im[:, -1]


def hier_moe(h, w_rg, b_rg, w_re, b_re, w_gate, w_up, w_down):
    shp = h.shape
    t = h.reshape(-1, D_MODEL)
    p_group = jax.nn.softmax((t @ w_rg).astype(jnp.float32) + b_rg.astype(jnp.float32), axis=-1)
    g_val, g_idx = lax.top_k(p_group, 1)
    e_logits = ((t @ w_re).astype(jnp.float32) + b_re.astype(jnp.float32)).reshape(
        -1, N_EXPERT_GROUPS, EXPERTS_PER_GROUP)
    e_logits = jnp.take_along_axis(e_logits, g_idx[:, :, None], axis=1)[:, 0]
    e_val, e_idx = lax.top_k(jax.nn.softmax(e_logits, axis=-1), TOP_K_IN_GROUP)
    w = g_val * e_val / jnp.sum(e_val, axis=-1, keepdims=True)
    eid = g_idx * EXPERTS_PER_GROUP + e_idx
    combine = jnp.sum(jax.nn.one_hot(eid, N_EXPERTS, dtype=jnp.float32) * w[..., None], axis=1)
    combine = combine.astype(t.dtype)
    y = jnp.zeros_like(t)
    for e in range(N_EXPERTS):
        hid = jax.nn.silu(t @ w_gate[e]) * (t @ w_up[e])
        y = y + combine[:, e:e + 1] * (hid @ w_down[e])
    return y.reshape(shp)


def setup_inputs(seed: int = 0) -> dict:
    key = jax.random.key(seed)
    ks = jax.random.split(key, 40)
    nrm = lambda k, shape, s: jax.random.normal(k, shape, jnp.float32) * s
    NA, NB = N_ATTN_LAYERS, N_SSM_LAYERS
    G, P, C = N_SSM_GROUPS, SSM_STATE, SSM_GROUP
    n = jnp.arange(P, dtype=jnp.float32)
    D = D_MODEL
    return {
        'x_prompt': nrm(ks[0], (BATCH, SEQ, D), 1.0),
        'x_sample': nrm(ks[1], (DEC_BATCH, DEC_SEQ, D), 1.0),
        'cache_k': nrm(ks[2], (NA, DEC_BATCH, PAST_LEN, N_HEADS, 2 * HEAD_DIM), 1.0),
        'cache_v': nrm(ks[3], (NA, DEC_BATCH, PAST_LEN, N_HEADS, V_DIM), 1.0),
        'state_ssm_re': nrm(ks[4], (NB, DEC_BATCH, G, P), 0.1),
        'state_ssm_im': nrm(ks[5], (NB, DEC_BATCH, G, P), 0.1),
        'norm_mix': 1.0 + nrm(ks[6], (DEPTH, D), 0.01),
        'norm_ffn': 1.0 + nrm(ks[7], (DEPTH, D), 0.01),
        'norm_final': 1.0 + nrm(ks[8], (D,), 0.01),
        'attn_w_qkv': nrm(ks[9], (NA, D, 3 * D), D ** -0.5),
        'attn_lambda_q1': nrm(ks[10], (NA, HEAD_DIM), 0.1),
        'attn_lambda_k1': nrm(ks[11], (NA, HEAD_DIM), 0.1),
        'attn_lambda_q2': nrm(ks[12], (NA, HEAD_DIM), 0.1),
        'attn_lambda_k2': nrm(ks[13], (NA, HEAD_DIM), 0.1),
        'attn_subln': 1.0 + nrm(ks[14], (NA, V_DIM), 0.01),
        'attn_w_o': nrm(ks[15], (NA, D, D), D ** -0.5),
        'ssm_lambda_re': -0.5 + nrm(ks[16], (NB, G, P), 0.01),
        'ssm_lambda_im': math.pi * n + nrm(ks[17], (NB, G, P), 0.01),
        'ssm_log_dt': jax.random.uniform(ks[18], (NB, G), jnp.float32, math.log(1e-3), math.log(1e-1)),
        'ssm_b_re': nrm(ks[19], (NB, G, P, C), (2 * C) ** -0.5),
        'ssm_b_im': nrm(ks[20], (NB, G, P, C), (2 * C) ** -0.5),
        'ssm_c_re': nrm(ks[21], (NB, G, C, P), 0.5),
        'ssm_c_im': nrm(ks[22], (NB, G, C, P), 0.5),
        'ssm_d': nrm(ks[23], (NB, D), 1.0),
        'ssm_w_glu': nrm(ks[24], (NB, D, D), D ** -0.5),
        'ssm_w_gate': nrm(ks[25], (NB, D, D), D ** -0.5),
        'moe_w_router_group': nrm(ks[26], (DEPTH, D, N_EXPERT_GROUPS), D ** -0.5),
        'moe_b_router_group': nrm(ks[27], (DEPTH, N_EXPERT_GROUPS), 0.01),
        'moe_w_router_expert': nrm(ks[28], (DEPTH, D, N_EXPERTS), D ** -0.5),
        'moe_b_router_expert': nrm(ks[29], (DEPTH, N_EXPERTS), 0.01),
        'moe_w_gate': nrm(ks[30], (DEPTH, N_EXPERTS, D, D_EXPERT), D ** -0.5),
        'moe_w_up': nrm(ks[31], (DEPTH, N_EXPERTS, D, D_EXPERT), D ** -0.5),
        'moe_w_down': nrm(ks[32], (DEPTH, N_EXPERTS, D_EXPERT, D), D_EXPERT ** -0.5),
    }


def reference(x_prompt, x_sample, cache_k, cache_v, state_ssm_re, state_ssm_im,
              norm_mix, norm_ffn, norm_final,
              attn_w_qkv, attn_lambda_q1, attn_lambda_k1, attn_lambda_q2, attn_lambda_k2,
              attn_subln, attn_w_o,
              ssm_lambda_re, ssm_lambda_im, ssm_log_dt, ssm_b_re, ssm_b_im, ssm_c_re, ssm_c_im,
              ssm_d, ssm_w_glu, ssm_w_gate,
              moe_w_router_group, moe_b_router_group, moe_w_router_expert, moe_b_router_expert,
              moe_w_gate, moe_w_up, moe_w_down):
    xp, xs = x_prompt, x_sample
    k_p, v_p, k_s, v_s = [], [], [], []
    sr_p, si_p, sr_s, si_s = [], [], [], []
    for i in range(DEPTH):
        j = i // N_MIXERS
        hp = rmsnorm(xp, norm_mix[i])
        hs = rmsnorm(xs, norm_mix[i])
        if i % N_MIXERS == 0:
            lam_init = lambda_init_for(i)
            attn = (attn_w_qkv[j], attn_lambda_q1[j], attn_lambda_k1[j], attn_lambda_q2[j],
                    attn_lambda_k2[j], attn_subln[j], attn_w_o[j])
            yp, kp, vp = diff_attn_prompt(hp, *attn, lam_init)
            ys, kn, vn = diff_attn_sample(hs, cache_k[j], cache_v[j], *attn, lam_init)
            k_p.append(kp); v_p.append(vp); k_s.append(kn); v_s.append(vn)
        else:
            ssm = (ssm_lambda_re[j], ssm_lambda_im[j], ssm_log_dt[j], ssm_b_re[j], ssm_b_im[j],
                   ssm_c_re[j], ssm_c_im[j], ssm_d[j], ssm_w_glu[j], ssm_w_gate[j])
            yp, rp, ip = s5_mixer(hp, None, None, *ssm)
            ys, rn, im_n = s5_mixer(hs, state_ssm_re[j], state_ssm_im[j], *ssm)
            sr_p.append(rp); si_p.append(ip); sr_s.append(rn); si_s.append(im_n)
        xp = xp + yp
        xs = xs + ys
        moe = (moe_w_router_group[i], moe_b_router_group[i], moe_w_router_expert[i],
               moe_b_router_expert[i], moe_w_gate[i], moe_w_up[i], moe_w_down[i])
        xp = xp + hier_moe(rmsnorm(xp, norm_ffn[i]), *moe)
        xs = xs + hier_moe(rmsnorm(xs, norm_ffn[i]), *moe)
    y_prompt = rmsnorm(xp, norm_final)
    y_sample = rmsnorm(xs, norm_final)
    return (y_prompt, y_sample,
            jnp.stack(k_p), jnp.stack(v_p), jnp.stack(sr_p), jnp.stack(si_p),
            jnp.stack(k_s), jnp.stack(v_s), jnp.stack(sr_s), jnp.stack(si_s))
```

```python
import functools
import math

import jax
import jax.numpy as jnp
from jax import lax
from jax.experimental import pallas as pl
from jax.experimental.pallas import tpu as pltpu

F32 = jnp.float32
BF16 = jnp.bfloat16

D_MODEL = 1024
CHUNK = 64
N_HEADS = 8
HEAD_DIM = 64
V_DIM = 2 * HEAD_DIM
ROT_DIM = HEAD_DIM // 4
ROPE_THETA = 500000.0
SSM_GROUP = 16
N_SSM_GROUPS = D_MODEL // SSM_GROUP
SSM_STATE = 64
N_STATES = N_SSM_GROUPS * SSM_STATE
N_EXPERT_GROUPS = 4
EXPERTS_PER_GROUP = 4
N_EXPERTS = N_EXPERT_GROUPS * EXPERTS_PER_GROUP
D_EXPERT = D_MODEL // 4
RMS_EPS = 1e-6

LANES = 128
SUBLANES = 8
VMEM_LIMIT = 56 * 1024 * 1024

SSM_BLK_CH = 256
SSM_NBLK = D_MODEL // SSM_BLK_CH
SSM_BLK_ST = (SSM_BLK_CH // SSM_GROUP) * SSM_STATE
SSM_STEPS = 32
SSM_ROWS = SUBLANES * SSM_STEPS


def _cparams(*sem):
    return pltpu.CompilerParams(dimension_semantics=sem, vmem_limit_bytes=VMEM_LIMIT)


def _rms(x, w):
    return x * lax.rsqrt(jnp.mean(x * x, axis=-1, keepdims=True) + RMS_EPS) * w


def _qkv_kernel(x_ref, nw_ref, w_ref, cos_ref, sina_ref, sinb_ref,
                q_ref, k_ref, v_ref, kb_ref, vb_ref):
    h = _rms(x_ref[...], nw_ref[...]).astype(BF16)
    cosf, sina, sinb = cos_ref[...], sina_ref[...], sinb_ref[...]

    def rope(t):
        outs = []
        for hh in range(N_HEADS):
            xs = t[:, hh * LANES:(hh + 1) * LANES]
            outs.append(xs * cosf + pltpu.roll(xs, LANES - ROT_DIM // 2, 1) * sina
                        + pltpu.roll(xs, ROT_DIM // 2, 1) * sinb)
        return jnp.concatenate(outs, axis=1)

    q = rope(jnp.dot(h, w_ref[:, 0:D_MODEL], preferred_element_type=F32))
    q_ref[...] = (q * (HEAD_DIM ** -0.5)).astype(BF16)
    k = rope(jnp.dot(h, w_ref[:, D_MODEL:2 * D_MODEL], preferred_element_type=F32))
    k_ref[...] = k
    kb_ref[...] = k.astype(BF16)
    v = jnp.dot(h, w_ref[:, 2 * D_MODEL:3 * D_MODEL], preferred_element_type=F32)
    v_ref[...] = v
    vb_ref[...] = v.astype(BF16)


def _qkv(x, nw, w, cosf, sina, sinb, tm):
    n = x.shape[0]
    row = lambda i: (i, 0)
    fix = lambda i: (0, 0)
    return pl.pallas_call(
        _qkv_kernel,
        grid=(n // tm,),
        in_specs=[pl.BlockSpec((tm, D_MODEL), row), pl.BlockSpec((1, D_MODEL), fix),
                  pl.BlockSpec((D_MODEL, 3 * D_MODEL), fix),
                  pl.BlockSpec((tm, LANES), row), pl.BlockSpec((tm, LANES), row),
                  pl.BlockSpec((tm, LANES), row)],
        out_specs=[pl.BlockSpec((tm, D_MODEL), row)] * 5,
        out_shape=[jax.ShapeDtypeStruct((n, D_MODEL), BF16),
                   jax.ShapeDtypeStruct((n, D_MODEL), F32),
                   jax.ShapeDtypeStruct((n, D_MODEL), F32),
                   jax.ShapeDtypeStruct((n, D_MODEL), BF16),
                   jax.ShapeDtypeStruct((n, D_MODEL), BF16)],
        compiler_params=_cparams("arbitrary"),
        name="qkv_rope",
    )(x, nw, w, cosf, sina, sinb)


def _stack_subheads(q):
    lane = lax.broadcasted_iota(jnp.int32, q.shape, 1)
    zero = jnp.zeros_like(q)
    return jnp.concatenate([jnp.where(lane < HEAD_DIM, q, zero),
                            jnp.where(lane >= HEAD_DIM, q, zero)], axis=0)


def _scores(qs, k):
    return lax.dot_general(qs, k, (((1,), (1,)), ((), ())), preferred_element_type=F32)


def _diff_finish(acc, l, t, lam, w, out_scale):
    o = acc[:t] / l[:t] - lam * (acc[t:] / l[t:])
    return _rms(o, w) * out_scale


def _attn_prompt_kernel(q_ref, k_ref, v_ref, w_ref, lam_ref, o_ref, *, t, out_scale):
    i = pl.program_id(1)
    qs = _stack_subheads(q_ref[...])

    def update(carry, s, vblk):
        m, l, acc = carry
        m_new = jnp.maximum(m, jnp.max(s, axis=-1, keepdims=True))
        alpha = jnp.exp(m - m_new)
        p = jnp.exp(s - m_new)
        l = alpha * l + jnp.sum(p, axis=-1, keepdims=True)
        acc = alpha * acc + jnp.dot(p.astype(BF16), vblk, preferred_element_type=F32)
        return m_new, l, acc

    def full_block(j, carry):
        r0 = pl.multiple_of(j * t, t)
        s = _scores(qs, k_ref[pl.ds(r0, t), :])
        return update(carry, s, v_ref[pl.ds(r0, t), :])

    init = (jnp.full((2 * t, 1), -jnp.inf, F32), jnp.zeros((2 * t, 1), F32),
            jnp.zeros((2 * t, V_DIM), F32))
    carry = lax.fori_loop(0, i, full_block, init)

    r0 = pl.multiple_of(i * t, t)
    s = _scores(qs, k_ref[pl.ds(r0, t), :])
    qc = (lax.broadcasted_iota(jnp.int32, (2 * t, t), 0) % t) // CHUNK
    kc = lax.broadcasted_iota(jnp.int32, (2 * t, t), 1) // CHUNK
    s = jnp.where(kc <= qc, s, -jnp.inf)
    _, l, acc = update(carry, s, v_ref[pl.ds(r0, t), :])
    o_ref[...] = _diff_finish(acc, l, t, lam_ref[...], w_ref[...], out_scale).astype(BF16)


def _attn_prompt(q, kb, vb, w_subln, lam, out_scale, t):
    l = q.shape[0]
    kern = functools.partial(_attn_prompt_kernel, t=t, out_scale=out_scale)
    return pl.pallas_call(
        kern,
        grid=(N_HEADS, l // t),
        in_specs=[pl.BlockSpec((t, V_DIM), lambda h, i: (i, h)),
                  pl.BlockSpec((l, V_DIM), lambda h, i: (0, h)),
                  pl.BlockSpec((l, V_DIM), lambda h, i: (0, h)),
                  pl.BlockSpec((1, V_DIM), lambda h, i: (0, 0)),
                  pl.BlockSpec((1, V_DIM), lambda h, i: (0, 0))],
        out_specs=pl.BlockSpec((t, V_DIM), lambda h, i: (i, h)),
        out_shape=jax.ShapeDtypeStruct((l, D_MODEL), BF16),
        compiler_params=_cparams("arbitrary", "arbitrary"),
        name="attn_prompt",
    )(q, kb, vb, w_subln, lam)


def _attn_sample_kernel(q_ref, kn_ref, vn_ref, ck_ref, cv_ref, w_ref, lam_ref, o_ref,
                        *, t, out_scale):
    for hh in range(N_HEADS):
        cols = slice(hh * V_DIM, (hh + 1) * V_DIM)
        qs = _stack_subheads(q_ref[:, cols])
        kc = ck_ref[0, :, cols].astype(BF16)
        vc = cv_ref[0, :, cols].astype(BF16)
        s_c = _scores(qs, kc)
        s_n = _scores(qs, kn_ref[:, cols])
        m = jnp.maximum(jnp.max(s_c, axis=-1, keepdims=True), jnp.max(s_n, axis=-1, keepdims=True))
        p_c = jnp.exp(s_c - m)
        p_n = jnp.exp(s_n - m)
        l = jnp.sum(p_c, axis=-1, keepdims=True) + jnp.sum(p_n, axis=-1, keepdims=True)
        acc = (jnp.dot(p_c.astype(BF16), vc, preferred_element_type=F32)
               + jnp.dot(p_n.astype(BF16), vn_ref[:, cols], preferred_element_type=F32))
        o_ref[:, cols] = _diff_finish(acc, l, t, lam_ref[...], w_ref[...], out_scale).astype(BF16)


def _attn_sample(q, kb, vb, cache_k, cache_v, w_subln, lam, out_scale, t):
    nb, past, _ = cache_k.shape
    kern = functools.partial(_attn_sample_kernel, t=t, out_scale=out_scale)
    row = lambda b: (b, 0)
    return pl.pallas_call(
        kern,
        grid=(nb,),
        in_specs=[pl.BlockSpec((t, D_MODEL), row), pl.BlockSpec((t, D_MODEL), row),
                  pl.BlockSpec((t, D_MODEL), row),
                  pl.BlockSpec((1, past, D_MODEL), lambda b: (b, 0, 0)),
                  pl.BlockSpec((1, past, D_MODEL), lambda b: (b, 0, 0)),
                  pl.BlockSpec((1, V_DIM), lambda b: (0, 0)),
                  pl.BlockSpec((1, V_DIM), lambda b: (0, 0))],
        out_specs=pl.BlockSpec((t, D_MODEL), row),
        out_shape=jax.ShapeDtypeStruct((nb * t, D_MODEL), BF16),
        compiler_params=_cparams("arbitrary"),
        name="attn_sample",
    )(q, kb, vb, cache_k, cache_v, w_subln, lam)


def _route(x, nw, wr, br):
    h = _rms(x, nw).astype(BF16)
    lg = jnp.dot(h, wr, preferred_element_type=F32) + br
    lane = lax.broadcasted_iota(jnp.int32, lg.shape, 1)
    big = jnp.int32(LANES)
    neg = jnp.float32(-jnp.inf)

    def masked_softmax(mask):
        z = jnp.where(mask, lg, neg)
        e = jnp.exp(z - jnp.max(z, axis=-1, keepdims=True))
        return e / jnp.sum(e, axis=-1, keepdims=True)

    def top1(p, mask):
        v = jnp.max(jnp.where(mask, p, -1.0), axis=-1, keepdims=True)
        idx = jnp.min(jnp.where(mask & (p == v), lane, big), axis=-1, keepdims=True)
        return v, idx

    gmask = lane < N_EXPERT_GROUPS
    g_val, g_idx = top1(masked_softmax(gmask), gmask)
    lo = N_EXPERT_GROUPS + g_idx * EXPERTS_PER_GROUP
    emask = (lane >= lo) & (lane < lo + EXPERTS_PER_GROUP)
    pe = masked_softmax(emask)
    v1, i1 = top1(pe, emask)
    v2, i2 = top1(pe, emask & (lane != i1))
    den = v1 + v2
    w1 = g_val * v1 / den
    w2 = g_val * v2 / den
    comb = (jnp.where(lane == i1 - N_EXPERT_GROUPS, w1, 0.0)
            + jnp.where(lane == i2 - N_EXPERT_GROUPS, w2, 0.0))
    return h, comb


def _wo_kernel(o_ref, x_ref, wo_ref, x1_ref):
    x1_ref[...] = x_ref[...] + jnp.dot(o_ref[...], wo_ref[...], preferred_element_type=F32)


def _wo(o, x, wo, tm):
    n = x.shape[0]
    row = lambda i: (i, 0)
    return pl.pallas_call(
        _wo_kernel,
        grid=(n // tm,),
        in_specs=[pl.BlockSpec((tm, D_MODEL), row), pl.BlockSpec((tm, D_MODEL), row),
                  pl.BlockSpec((D_MODEL, D_MODEL), lambda i: (0, 0))],
        out_specs=pl.BlockSpec((tm, D_MODEL), row),
        out_shape=jax.ShapeDtypeStruct((n, D_MODEL), F32),
        compiler_params=_cparams("arbitrary"),
        name="attn_out_proj",
    )(o, x, wo)


def _moe_kernel(x_ref, nfw_ref, wr_ref, br_ref, wg_ref, wu_ref, wd_ref, nw_ref, y_ref,
                h_ref, c_ref, *, final_norm):
    g = pl.program_id(1)

    @pl.when(g == 0)
    def _():
        x = x_ref[...]
        y_ref[...] = x
        h_ref[...], c_ref[...] = _route(x, nfw_ref[...], wr_ref[...], br_ref[...])

    h = h_ref[...]
    comb = c_ref[...]
    acc = y_ref[...]
    for e in range(EXPERTS_PER_GROUP):
        gate = jnp.dot(h, wg_ref[e], preferred_element_type=F32)
        up = jnp.dot(h, wu_ref[e], preferred_element_type=F32)
        hid = (gate * jax.nn.sigmoid(gate) * up).astype(BF16)
        lane = lax.broadcasted_iota(jnp.int32, comb.shape, 1)
        ce = jnp.sum(jnp.where(lane == g * EXPERTS_PER_GROUP + e, comb, 0.0),
                     axis=-1, keepdims=True)
        acc = acc + ce * jnp.dot(hid, wd_ref[e], preferred_element_type=F32)
    y_ref[...] = acc

    if final_norm:
        @pl.when(g == N_EXPERT_GROUPS - 1)
        def _():
            y_ref[...] = _rms(y_ref[...], nw_ref[...])


def _moe(x, nfw, wr, br, wg, wu, wd, nw, final_norm, tm):
    n = x.shape[0]
    row = lambda i, g: (i, 0)
    fix = lambda i, g: (0, 0)
    kern = functools.partial(_moe_kernel, final_norm=final_norm)
    return pl.pallas_call(
        kern,
        grid=(n // tm, N_EXPERT_GROUPS),
        in_specs=[pl.BlockSpec((tm, D_MODEL), row), pl.BlockSpec((1, D_MODEL), fix),
                  pl.BlockSpec((D_MODEL, LANES), fix), pl.BlockSpec((1, LANES), fix),
                  pl.BlockSpec((EXPERTS_PER_GROUP, D_MODEL, D_EXPERT), lambda i, g: (g, 0, 0)),
                  pl.BlockSpec((EXPERTS_PER_GROUP, D_MODEL, D_EXPERT), lambda i, g: (g, 0, 0)),
                  pl.BlockSpec((EXPERTS_PER_GROUP, D_EXPERT, D_MODEL), lambda i, g: (g, 0, 0)),
                  pl.BlockSpec((1, D_MODEL), lambda i, g: (0, 0))],
        out_specs=pl.BlockSpec((tm, D_MODEL), row),
        out_shape=jax.ShapeDtypeStruct((n, D_MODEL), F32),
        scratch_shapes=[pltpu.VMEM((tm, D_MODEL), BF16), pltpu.VMEM((tm, LANES), F32)],
        compiler_params=_cparams("arbitrary", "arbitrary"),
        name="moe",
    )(x, nfw, wr, br, wg, wu, wd, nw)


def _cmul(ar, ai, br, bi):
    return ar * br - ai * bi, ar * bi + ai * br


def _s5_kernel(x_ref, x0r_ref, x0i_ref, nmw_ref, ar_ref, ai_ref, bm_ref, cm_ref, d_ref,
               wglu_ref, wgate_ref,
               x2_ref, sr_ref, si_ref,
               xp_ref, bu_ref, y_ref, pr_ref, pi_ref, cr_ref, ci_ref, car_r_ref, car_i_ref,
               *, chain):
    steps, rows, nst = SSM_STEPS, SSM_ROWS, SSM_BLK_ST

    @pl.when(pl.program_id(0) == 0)
    def _():
        for kb in range(SSM_NBLK):
            ar, ai = ar_ref[kb], ai_ref[kb]
            pr, pi = ar, ai
            pr_ref[kb, 0:1, :] = pr
            pi_ref[kb, 0:1, :] = pi
            for i in range(1, steps):
                pr, pi = _cmul(ar, ai, pr, pi)
                pr_ref[kb, i:i + 1, :] = pr
                pi_ref[kb, i:i + 1, :] = pi
        car_r_ref[...] = jnp.zeros_like(car_r_ref)
        car_i_ref[...] = jnp.zeros_like(car_i_ref)

    nlb = D_MODEL // LANES
    for i in range(steps):
        for b in range(nlb):
            xp_ref[i * SUBLANES:(i + 1) * SUBLANES, b * LANES:(b + 1) * LANES] = (
                x_ref[pl.ds(i * nlb + b, SUBLANES, stride=steps * nlb), :])

    u = _rms(xp_ref[...], nmw_ref[...])
    ub = u.astype(BF16)

    half = nst // 2
    for kb in range(SSM_NBLK):
        bu_ref[...] = jnp.dot(ub[:, kb * SSM_BLK_CH:(kb + 1) * SSM_BLK_CH], bm_ref[kb],
                              preferred_element_type=F32)
        for hh in range(2):
            re = slice(hh * half, (hh + 1) * half)
            im = slice(nst + hh * half, nst + (hh + 1) * half)
            ar = jnp.broadcast_to(ar_ref[kb, :, re], (SUBLANES, half))
            ai = jnp.broadcast_to(ai_ref[kb, :, re], (SUBLANES, half))

            def scan_step(i, s, re=re, im=im, ar=ar, ai=ai):
                sr, si = s
                r0 = pl.multiple_of(i * SUBLANES, SUBLANES)
                tr, ti = _cmul(ar, ai, sr, si)
                sr = tr + bu_ref[pl.ds(r0, SUBLANES), re]
                si = ti + bu_ref[pl.ds(r0, SUBLANES), im]
                bu_ref[pl.ds(r0, SUBLANES), re] = sr
                bu_ref[pl.ds(r0, SUBLANES), im] = si
                return sr, si

            zero = jnp.zeros((SUBLANES, half), F32)
            lax.fori_loop(0, steps, scan_step, (zero, zero), unroll=4)

        if chain:
            alr, ali = pr_ref[kb, steps - 1:steps, :], pi_ref[kb, steps - 1:steps, :]
            cr, ci = car_r_ref[kb], car_i_ref[kb]
            for j in range(SUBLANES):
                cr_ref[j:j + 1, :] = cr
                ci_ref[j:j + 1, :] = ci
                last = rows - SUBLANES + j
                tr, ti = _cmul(alr, ali, cr, ci)
                cr = tr + bu_ref[last:last + 1, 0:nst]
                ci = ti + bu_ref[last:last + 1, nst:2 * nst]
            car_r_ref[kb] = cr
            car_i_ref[kb] = ci
        else:
            cr_ref[...] = x0r_ref[:, kb * nst:(kb + 1) * nst]
            ci_ref[...] = x0i_ref[:, kb * nst:(kb + 1) * nst]

        c_r, c_i = cr_ref[...], ci_ref[...]

        def fix_step(i, _, kb=kb, c_r=c_r, c_i=c_i):
            r0 = pl.multiple_of(i * SUBLANES, SUBLANES)
            tr, ti = _cmul(pr_ref[kb, pl.ds(i, 1), :], pi_ref[kb, pl.ds(i, 1), :], c_r, c_i)
            bu_ref[pl.ds(r0, SUBLANES), 0:nst] = bu_ref[pl.ds(r0, SUBLANES), 0:nst] + tr
            bu_ref[pl.ds(r0, SUBLANES), nst:2 * nst] = bu_ref[pl.ds(r0, SUBLANES), nst:2 * nst] + ti
            return 0

        lax.fori_loop(0, steps, fix_step, 0, unroll=4)
        sr_ref[:, kb * nst:(kb + 1) * nst] = bu_ref[rows - SUBLANES:rows, 0:nst]
        si_ref[:, kb * nst:(kb + 1) * nst] = bu_ref[rows - SUBLANES:rows, nst:2 * nst]
        y_ref[:, kb * SSM_BLK_CH:(kb + 1) * SSM_BLK_CH] = jnp.dot(
            bu_ref[...].astype(BF16), cm_ref[kb], preferred_element_type=F32)

    y = y_ref[...] + d_ref[...] * u
    z = jax.nn.gelu(y).astype(BF16)
    out = (jnp.dot(z, wglu_ref[...], preferred_element_type=F32)
           * jax.nn.sigmoid(jnp.dot(z, wgate_ref[...], preferred_element_type=F32)))
    y_ref[...] = xp_ref[...] + out
    for i in range(steps):
        for b in range(nlb):
            x2_ref[pl.ds(i * nlb + b, SUBLANES, stride=steps * nlb), :] = (
                y_ref[i * SUBLANES:(i + 1) * SUBLANES, b * LANES:(b + 1) * LANES])


def _s5(x, x0r, x0i, nmw, a_re, a_im, bm, cm, d, wglu, wgate, chain):
    n = x.shape[0]
    nlb = D_MODEL // LANES
    nchunk = n // SSM_ROWS
    row = lambda i: (i, 0)
    fix2 = lambda i: (0, 0)
    fix3 = lambda i: (0, 0, 0)
    state_map = fix2 if chain else row
    n_state_rows = SUBLANES if chain else SUBLANES * nchunk
    kern = functools.partial(_s5_kernel, chain=chain)
    return pl.pallas_call(
        kern,
        grid=(nchunk,),
        in_specs=[pl.BlockSpec((SSM_ROWS * nlb, LANES), row),
                  pl.BlockSpec((SUBLANES, N_STATES), state_map),
                  pl.BlockSpec((SUBLANES, N_STATES), state_map),
                  pl.BlockSpec((1, D_MODEL), fix2),
                  pl.BlockSpec((SSM_NBLK, 1, SSM_BLK_ST), fix3),
                  pl.BlockSpec((SSM_NBLK, 1, SSM_BLK_ST), fix3),
                  pl.BlockSpec((SSM_NBLK, SSM_BLK_CH, 2 * SSM_BLK_ST), fix3),
                  pl.BlockSpec((SSM_NBLK, 2 * SSM_BLK_ST, SSM_BLK_CH), fix3),
                  pl.BlockSpec((1, D_MODEL), fix2),
                  pl.BlockSpec((D_MODEL, D_MODEL), fix2),
                  pl.BlockSpec((D_MODEL, D_MODEL), fix2)],
        out_specs=[pl.BlockSpec((SSM_ROWS * nlb, LANES), row),
                   pl.BlockSpec((SUBLANES, N_STATES), state_map),
                   pl.BlockSpec((SUBLANES, N_STATES), state_map)],
        out_shape=[jax.ShapeDtypeStruct((n * nlb, LANES), F32),
                   jax.ShapeDtypeStruct((n_state_rows, N_STATES), F32),
                   jax.ShapeDtypeStruct((n_state_rows, N_STATES), F32)],
        scratch_shapes=[pltpu.VMEM((SSM_ROWS, D_MODEL), F32),
                        pltpu.VMEM((SSM_ROWS, 2 * SSM_BLK_ST), F32),
                        pltpu.VMEM((SSM_ROWS, D_MODEL), F32),
                        pltpu.VMEM((SSM_NBLK, SSM_STEPS, SSM_BLK_ST), F32),
                        pltpu.VMEM((SSM_NBLK, SSM_STEPS, SSM_BLK_ST), F32),
                        pltpu.VMEM((SUBLANES, SSM_BLK_ST), F32),
                        pltpu.VMEM((SUBLANES, SSM_BLK_ST), F32),
                        pltpu.VMEM((SSM_NBLK, 1, SSM_BLK_ST), F32),
                        pltpu.VMEM((SSM_NBLK, 1, SSM_BLK_ST), F32)],
        compiler_params=_cparams("arbitrary"),
        name="s5_mixer",
    )(x.reshape(n * nlb, LANES), x0r, x0i, nmw, a_re, a_im, bm, cm, d, wglu, wgate)


def _rope_tables(pos):
    half = ROT_DIM // 2
    inv_freq = ROPE_THETA ** (-jnp.arange(half, dtype=F32) * 2.0 / ROT_DIM)
    ang = pos.astype(F32)[:, None] * inv_freq[None, :]
    cos, sin = jnp.cos(ang), jnp.sin(ang)
    n = pos.shape[0]
    pad = HEAD_DIM - ROT_DIM
    cosf = jnp.concatenate([cos, cos, jnp.ones((n, pad), F32)], axis=1)
    sina = jnp.concatenate([-sin, jnp.zeros((n, half + pad), F32)], axis=1)
    sinb = jnp.concatenate([jnp.zeros((n, half), F32), sin, jnp.zeros((n, pad), F32)], axis=1)
    tile2 = lambda t: jnp.concatenate([t, t], axis=1)
    return tile2(cosf), tile2(sina), tile2(sinb)


def _s5_params(lam_re, lam_im, log_dt, b_re, b_im, c_re, c_im):
    dt = jnp.exp(log_dt)[:, None]
    z_re, z_im = lam_re * dt, lam_im * dt
    mag = jnp.exp(z_re)
    lb_re, lb_im = mag * jnp.cos(z_im), mag * jnp.sin(z_im)
    n_re, n_im = lb_re - 1.0, lb_im
    den = lam_re * lam_re + lam_im * lam_im
    k_re = (n_re * lam_re + n_im * lam_im) / den
    k_im = (n_im * lam_re - n_re * lam_im) / den
    bb_re = k_re[..., None] * b_re - k_im[..., None] * b_im
    bb_im = k_re[..., None] * b_im + k_im[..., None] * b_re
    gpb = SSM_BLK_CH // SSM_GROUP
    eye = jnp.eye(gpb, dtype=F32)

    def in_blocks(bb):
        v = bb.reshape(SSM_NBLK, gpb, SSM_STATE, SSM_GROUP).transpose(0, 1, 3, 2)
        return jnp.einsum('kgcp,gh->kgchp', v, eye).reshape(SSM_NBLK, SSM_BLK_CH, SSM_BLK_ST)

    def out_blocks(cc):
        v = cc.reshape(SSM_NBLK, gpb, SSM_GROUP, SSM_STATE).transpose(0, 1, 3, 2)
        return jnp.einsum('kgpc,gh->kgphc', v, eye).reshape(SSM_NBLK, SSM_BLK_ST, SSM_BLK_CH)

    bm = jnp.concatenate([in_blocks(bb_re), in_blocks(bb_im)], axis=2).astype(BF16)
    cm = jnp.concatenate([out_blocks(c_re), -out_blocks(c_im)], axis=1).astype(BF16)
    a_re = lb_re.reshape(SSM_NBLK, 1, SSM_BLK_ST)
    a_im = lb_im.reshape(SSM_NBLK, 1, SSM_BLK_ST)
    return a_re, a_im, bm, cm


def _router_params(w_rg, b_rg, w_re, b_re):
    pad = LANES - N_EXPERT_GROUPS - N_EXPERTS
    wr = jnp.concatenate([w_rg, w_re, jnp.zeros((D_MODEL, pad), F32)], axis=1).astype(BF16)
    br = jnp.concatenate([b_rg, b_re, jnp.zeros((pad,), F32)])[None, :]
    return wr, br


def kernel(x_prompt, x_sample, cache_k, cache_v, state_ssm_re, state_ssm_im, norm_mix, norm_ffn, norm_final, attn_w_qkv, attn_lambda_q1, attn_lambda_k1, attn_lambda_q2, attn_lambda_k2, attn_subln, attn_w_o, ssm_lambda_re, ssm_lambda_im, ssm_log_dt, ssm_b_re, ssm_b_im, ssm_c_re, ssm_c_im, ssm_d, ssm_w_glu, ssm_w_gate, moe_w_router_group, moe_b_router_group, moe_w_router_expert, moe_b_router_expert, moe_w_gate, moe_w_up, moe_w_down):
    _, seq, _ = x_prompt.shape
    nb, dec, _ = x_sample.shape
    past = cache_k.shape[2]
    xp = x_prompt.reshape(seq, D_MODEL)
    xs = x_sample.reshape(nb * dec, D_MODEL)

    lam_init = 0.8 - 0.6 * math.exp(-0.3 * 0)
    lam = (jnp.exp(jnp.sum(attn_lambda_q1[0] * attn_lambda_k1[0]))
           - jnp.exp(jnp.sum(attn_lambda_q2[0] * attn_lambda_k2[0])) + lam_init)
    lam = jnp.full((1, V_DIM), lam, F32)
    w_subln = attn_subln[0][None, :]
    out_scale = 1.0 - lam_init
    wqkv = attn_w_qkv[0].astype(BF16)
    wo = attn_w_o[0].astype(BF16)
    nm0 = norm_mix[0][None, :]
    tabs_p = _rope_tables(jnp.arange(seq, dtype=jnp.int32))
    tabs_s = tuple(jnp.tile(t, (nb, 1))
                   for t in _rope_tables(past + jnp.arange(dec, dtype=jnp.int32)))

    q_p, k_p, v_p, kb_p, vb_p = _qkv(xp, nm0, wqkv, *tabs_p, tm=512)
    q_s, k_s, v_s, kb_s, vb_s = _qkv(xs, nm0, wqkv, *tabs_s, tm=512)
    o_p = _attn_prompt(q_p, kb_p, vb_p, w_subln, lam, out_scale, t=512)
    o_s = _attn_sample(q_s, kb_s, vb_s, cache_k[0].reshape(nb, past, D_MODEL),
                       cache_v[0].reshape(nb, past, D_MODEL), w_subln, lam, out_scale, t=dec)

    nf0 = norm_ffn[0][None, :]
    wr0, br0 = _router_params(moe_w_router_group[0], moe_b_router_group[0],
                              moe_w_router_expert[0], moe_b_router_expert[0])
    x1_p = _wo(o_p, xp, wo, tm=512)
    x1_s = _wo(o_s, xs, wo, tm=512)
    wg0, wu0, wd0 = (moe_w_gate[0].astype(BF16), moe_w_up[0].astype(BF16),
                     moe_w_down[0].astype(BF16))
    nfin = norm_final[None, :]
    xp1 = _moe(x1_p, nf0, wr0, br0, wg0, wu0, wd0, nfin, False, tm=1024)
    xs1 = _moe(x1_s, nf0, wr0, br0, wg0, wu0, wd0, nfin, False, tm=512)

    a_re, a_im, bm, cm = _s5_params(ssm_lambda_re[0], ssm_lambda_im[0], ssm_log_dt[0],
                                    ssm_b_re[0], ssm_b_im[0], ssm_c_re[0], ssm_c_im[0])
    nm1 = norm_mix[1][None, :]
    nf1 = norm_ffn[1][None, :]
    wr1, br1 = _router_params(moe_w_router_group[1], moe_b_router_group[1],
                              moe_w_router_expert[1], moe_b_router_expert[1])
    s5w = (nm1, a_re, a_im, bm, cm, ssm_d[0][None, :], ssm_w_glu[0].astype(BF16),
           ssm_w_gate[0].astype(BF16))
    zero_state = jnp.zeros((SUBLANES, N_STATES), F32)
    x2_p, sr_p, si_p = _s5(xp1, zero_state, zero_state, *s5w, chain=True)
    x0r = state_ssm_re[0].reshape(nb, N_STATES)
    x0i = state_ssm_im[0].reshape(nb, N_STATES)
    x2_s, sr_s, si_s = _s5(xs1, x0r, x0i, *s5w, chain=False)

    wg1, wu1, wd1 = (moe_w_gate[1].astype(BF16), moe_w_up[1].astype(BF16),
                     moe_w_down[1].astype(BF16))
    y_p = _moe(x2_p.reshape(seq, D_MODEL), nf1, wr1, br1, wg1, wu1, wd1, nfin, True, tm=1024)
    y_s = _moe(x2_s.reshape(nb * dec, D_MODEL), nf1, wr1, br1, wg1, wu1, wd1, nfin, True, tm=512)

    hshape = (N_HEADS, V_DIM)
    gshape = (N_SSM_GROUPS, SSM_STATE)
    return (y_p.reshape(1, seq, D_MODEL), y_s.reshape(nb, dec, D_MODEL),
            k_p.reshape(1, 1, seq, *hshape), v_p.reshape(1, 1, seq, *hshape),
            sr_p[SUBLANES - 1].reshape(1, 1, *gshape), si_p[SUBLANES - 1].reshape(1, 1, *gshape),
            k_s.reshape(1, nb, dec, *hshape), v_s.reshape(1, nb, dec, *hshape),
            sr_s.reshape(1, nb, *gshape), si_s.reshape(1, nb, *gshape))
```

```python
import functools
import math

import jax
import jax.numpy as jnp
from jax import lax
from jax.experimental import pallas as pl
from jax.experimental.pallas import tpu as pltpu

F32 = jnp.float32
BF16 = jnp.bfloat16

D_MODEL = 1024
CHUNK = 64
N_HEADS = 8
HEAD_DIM = 64
V_DIM = 2 * HEAD_DIM
ROT_DIM = HEAD_DIM // 4
ROPE_THETA = 500000.0
SSM_GROUP = 16
N_SSM_GROUPS = D_MODEL // SSM_GROUP
SSM_STATE = 64
N_STATES = N_SSM_GROUPS * SSM_STATE
N_EXPERT_GROUPS = 4
EXPERTS_PER_GROUP = 4
N_EXPERTS = N_EXPERT_GROUPS * EXPERTS_PER_GROUP
D_EXPERT = D_MODEL // 4
RMS_EPS = 1e-6

LANES = 128
SUBLANES = 8
VMEM_LIMIT = 56 * 1024 * 1024

SSM_BLK_CH = 256
SSM_NBLK = D_MODEL // SSM_BLK_CH
SSM_BLK_ST = (SSM_BLK_CH // SSM_GROUP) * SSM_STATE
SSM_STEPS = 32
SSM_ROWS = SUBLANES * SSM_STEPS


def _cparams(*sem):
    return pltpu.CompilerParams(dimension_semantics=sem, vmem_limit_bytes=VMEM_LIMIT)


def _rms(x, w):
    return x * lax.rsqrt(jnp.mean(x * x, axis=-1, keepdims=True) + RMS_EPS) * w


def _qkv_kernel(x_ref, nw_ref, w_ref, cos_ref, sina_ref, sinb_ref,
                q_ref, k_ref, v_ref, kb_ref, vb_ref):
    h = _rms(x_ref[...], nw_ref[...]).astype(BF16)
    cosf, sina, sinb = cos_ref[...], sina_ref[...], sinb_ref[...]

    def rope(t):
        outs = []
        for hh in range(N_HEADS):
            xs = t[:, hh * LANES:(hh + 1) * LANES]
            outs.append(xs * cosf + pltpu.roll(xs, LANES - ROT_DIM // 2, 1) * sina
                        + pltpu.roll(xs, ROT_DIM // 2, 1) * sinb)
        return jnp.concatenate(outs, axis=1)

    q = rope(jnp.dot(h, w_ref[:, 0:D_MODEL], preferred_element_type=F32))
    q_ref[...] = (q * (HEAD_DIM ** -0.5)).astype(BF16)
    k = rope(jnp.dot(h, w_ref[:, D_MODEL:2 * D_MODEL], preferred_element_type=F32))
    k_ref[...] = k
    kb_ref[...] = k.astype(BF16)
    v = jnp.dot(h, w_ref[:, 2 * D_MODEL:3 * D_MODEL], preferred_element_type=F32)
    v_ref[...] = v
    vb_ref[...] = v.astype(BF16)


def _qkv(x, nw, w, cosf, sina, sinb, tm):
    n = x.shape[0]
    row = lambda i: (i, 0)
    fix = lambda i: (0, 0)
    return pl.pallas_call(
        _qkv_kernel,
        grid=(n // tm,),
        in_specs=[pl.BlockSpec((tm, D_MODEL), row), pl.BlockSpec((1, D_MODEL), fix),
                  pl.BlockSpec((D_MODEL, 3 * D_MODEL), fix),
                  pl.BlockSpec((tm, LANES), row), pl.BlockSpec((tm, LANES), row),
                  pl.BlockSpec((tm, LANES), row)],
        out_specs=[pl.BlockSpec((tm, D_MODEL), row)] * 5,
        out_shape=[jax.ShapeDtypeStruct((n, D_MODEL), BF16),
                   jax.ShapeDtypeStruct((n, D_MODEL), F32),
                   jax.ShapeDtypeStruct((n, D_MODEL), F32),
                   jax.ShapeDtypeStruct((n, D_MODEL), BF16),
                   jax.ShapeDtypeStruct((n, D_MODEL), BF16)],
        compiler_params=_cparams("arbitrary"),
        name="qkv_rope",
    )(x, nw, w, cosf, sina, sinb)


ATT_T = 512
ATT_K = 256
ATT_ONES = 16


def _qkv_prompt_kernel(x_ref, nw_ref, wqt_ref, wk_ref, wv_ref, wvt_ref,
                       cos_ref, sina_ref, sinb_ref, cost_ref, sint_ref,
                       k_ref, v_ref, kb_ref, qt_ref, vt_ref):
    h = _rms(x_ref[...], nw_ref[...]).astype(BF16)
    cosf, sina, sinb = cos_ref[...], sina_ref[...], sinb_ref[...]
    outs = []
    k = jnp.dot(h, wk_ref[...], preferred_element_type=F32)
    for hh in range(N_HEADS):
        xs = k[:, hh * LANES:(hh + 1) * LANES]
        outs.append(xs * cosf + pltpu.roll(xs, LANES - ROT_DIM // 2, 1) * sina
                    + pltpu.roll(xs, ROT_DIM // 2, 1) * sinb)
    k = jnp.concatenate(outs, axis=1)
    k_ref[...] = k
    kb_ref[...] = k.astype(BF16)
    v_ref[...] = jnp.dot(h, wv_ref[...], preferred_element_type=F32)

    nt = (((1,), (1,)), ((), ()))
    vt = lax.dot_general(wvt_ref[...], h, nt, preferred_element_type=F32).astype(BF16)
    qt = lax.dot_general(wqt_ref[...], h, nt, preferred_element_type=F32) * (
        HEAD_DIM ** -0.5 * math.log2(math.e))
    cost, sint = cost_ref[...], sint_ref[...]
    half = ROT_DIM // 2
    pieces = []
    for blk in range(D_MODEL // HEAD_DIM):
        r0 = blk * HEAD_DIM
        x1, x2 = qt[r0:r0 + half], qt[r0 + half:r0 + ROT_DIM]
        pieces += [x1 * cost - x2 * sint, x2 * cost + x1 * sint, qt[r0 + ROT_DIM:r0 + HEAD_DIM]]
    qt = jnp.concatenate(pieces, axis=0).astype(BF16)
    for b in range(qt_ref.shape[0]):
        qt_ref[b] = qt[:, b * ATT_T:(b + 1) * ATT_T]
    for b in range(vt_ref.shape[0]):
        vt_ref[b] = vt[:, b * ATT_K:(b + 1) * ATT_K]


def _qkv_prompt(x, nw, wqt, wk, wv, wvt, cosf, sina, sinb, cost, sint, tm):
    n = x.shape[0]
    row = lambda i: (i, 0)
    fix = lambda i: (0, 0)
    wspec = pl.BlockSpec((D_MODEL, D_MODEL), fix)
    tspec = pl.BlockSpec((tm, LANES), row)
    ttspec = pl.BlockSpec((ROT_DIM // 2, tm), lambda i: (0, i))
    qblk = pl.BlockSpec((tm // ATT_T, D_MODEL, ATT_T), lambda i: (i, 0, 0))
    vblk = pl.BlockSpec((tm // ATT_K, D_MODEL, ATT_K), lambda i: (i, 0, 0))
    return pl.pallas_call(
        _qkv_prompt_kernel,
        grid=(n // tm,),
        in_specs=[pl.BlockSpec((tm, D_MODEL), row), pl.BlockSpec((1, D_MODEL), fix),
                  wspec, wspec, wspec, wspec, tspec, tspec, tspec, ttspec, ttspec],
        out_specs=[pl.BlockSpec((tm, D_MODEL), row)] * 3 + [qblk, vblk],
        out_shape=[jax.ShapeDtypeStruct((n, D_MODEL), F32),
                   jax.ShapeDtypeStruct((n, D_MODEL), F32),
                   jax.ShapeDtypeStruct((n, D_MODEL), BF16),
                   jax.ShapeDtypeStruct((n // ATT_T, D_MODEL, ATT_T), BF16),
                   jax.ShapeDtypeStruct((n // ATT_K, D_MODEL, ATT_K), BF16)],
        compiler_params=_cparams("arbitrary"),
        name="qkv_rope_prompt",
    )(x, nw, wqt, wk, wv, wvt, cosf, sina, sinb, cost, sint)


def _stack_subheads(q):
    lane = lax.broadcasted_iota(jnp.int32, q.shape, 1)
    zero = jnp.zeros_like(q)
    return jnp.concatenate([jnp.where(lane < HEAD_DIM, q, zero),
                            jnp.where(lane >= HEAD_DIM, q, zero)], axis=0)


def _scores(qs, k):
    return lax.dot_general(qs, k, (((1,), (1,)), ((), ())), preferred_element_type=F32)


def _diff_finish(acc, l, t, lam, w, out_scale):
    o = acc[:t] / l[:t] - lam * (acc[t:] / l[t:])
    return _rms(o, w) * out_scale


def _attn_prompt_kernel(qt_ref, k_ref, vt_ref, w_ref, lam_ref, o_ref, sa_ref, sb_ref, acc_ref,
                        *, out_scale):
    t, tk = ATT_T, ATT_K
    i = pl.program_id(1)
    qt = qt_ref[0]
    row = lax.broadcasted_iota(jnp.int32, qt.shape, 0)
    zero = jnp.zeros_like(qt)
    qs = jnp.concatenate([jnp.where(row < HEAD_DIM, qt, zero),
                          jnp.where(row >= HEAD_DIM, qt, zero)], axis=1)

    def scores(j):
        r0 = pl.multiple_of(j * tk, tk)
        return jnp.dot(k_ref[pl.ds(r0, tk), :], qs, preferred_element_type=F32)

    ones = jnp.ones((ATT_ONES, tk), BF16)

    def update(m, s, j):
        m_new = jnp.maximum(m, jnp.max(s, axis=0, keepdims=True))
        alpha = jnp.exp2(m - m_new)
        p = jnp.exp2(s - m_new).astype(BF16)
        v1 = jnp.concatenate([vt_ref[j], ones], axis=0)
        acc_ref[...] = alpha * acc_ref[...] + jnp.dot(v1, p, preferred_element_type=F32)
        return m_new

    def masked(s, j):
        kc = (j * tk + lax.broadcasted_iota(jnp.int32, (tk, 2 * t), 0)) // CHUNK
        qc = (i * t + lax.broadcasted_iota(jnp.int32, (tk, 2 * t), 1) % t) // CHUNK
        return jnp.where(kc <= qc, s, -jnp.inf)

    acc_ref[...] = jnp.zeros_like(acc_ref)
    sa_ref[...] = scores(0)

    def pair(jj, m):
        sb_ref[...] = scores(2 * jj + 1)
        m = update(m, sa_ref[...], 2 * jj)
        sa_ref[...] = scores(2 * jj + 2)
        return update(m, sb_ref[...], 2 * jj + 1)

    m = lax.fori_loop(0, i, pair, jnp.full((1, 2 * t), -jnp.inf, F32))

    sb_ref[...] = scores(2 * i + 1)
    m = update(m, masked(sa_ref[...], 2 * i), 2 * i)
    update(m, masked(sb_ref[...], 2 * i + 1), 2 * i + 1)

    acc = acc_ref[0:V_DIM, :]
    l = acc_ref[V_DIM:V_DIM + 1, :]
    o = acc[:, :t] / l[:, :t] - lam_ref[...] * (acc[:, t:] / l[:, t:])
    o = o * lax.rsqrt(jnp.mean(o * o, axis=0, keepdims=True) + RMS_EPS)
    o_ref[...] = (o.T * w_ref[...] * out_scale).astype(BF16)


def _attn_prompt(qt, kb, vt, w_subln, lam, out_scale):
    l = kb.shape[0]
    t, tk = ATT_T, ATT_K
    kern = functools.partial(_attn_prompt_kernel, out_scale=out_scale)
    return pl.pallas_call(
        kern,
        grid=(N_HEADS, l // t),
        in_specs=[pl.BlockSpec((1, V_DIM, t), lambda h, i: (i, h, 0)),
                  pl.BlockSpec((l, V_DIM), lambda h, i: (0, h)),
                  pl.BlockSpec((l // tk, V_DIM, tk), lambda h, i: (0, h, 0)),
                  pl.BlockSpec((1, V_DIM), lambda h, i: (0, 0)),
                  pl.BlockSpec((1, t), lambda h, i: (0, 0))],
        out_specs=pl.BlockSpec((t, V_DIM), lambda h, i: (i, h)),
        out_shape=jax.ShapeDtypeStruct((l, D_MODEL), BF16),
        scratch_shapes=[pltpu.VMEM((tk, 2 * t), F32), pltpu.VMEM((tk, 2 * t), F32),
                        pltpu.VMEM((V_DIM + ATT_ONES, 2 * t), F32)],
        compiler_params=_cparams("arbitrary", "arbitrary"),
        name="attn_prompt",
    )(qt, kb, vt, w_subln, lam)


def _attn_sample_kernel(q_ref, kn_ref, vn_ref, ck_ref, cv_ref, w_ref, lam_ref, o_ref,
                        *, t, out_scale):
    for hh in range(N_HEADS):
        cols = slice(hh * V_DIM, (hh + 1) * V_DIM)
        qs = _stack_subheads(q_ref[:, cols])
        kc = ck_ref[0, :, cols].astype(BF16)
        vc = cv_ref[0, :, cols].astype(BF16)
        s_c = _scores(qs, kc)
        s_n = _scores(qs, kn_ref[:, cols])
        m = jnp.maximum(jnp.max(s_c, axis=-1, keepdims=True), jnp.max(s_n, axis=-1, keepdims=True))
        p_c = jnp.exp(s_c - m)
        p_n = jnp.exp(s_n - m)
        l = jnp.sum(p_c, axis=-1, keepdims=True) + jnp.sum(p_n, axis=-1, keepdims=True)
        acc = (jnp.dot(p_c.astype(BF16), vc, preferred_element_type=F32)
               + jnp.dot(p_n.astype(BF16), vn_ref[:, cols], preferred_element_type=F32))
        o_ref[:, cols] = _diff_finish(acc, l, t, lam_ref[...], w_ref[...], out_scale).astype(BF16)


def _attn_sample(q, kb, vb, cache_k, cache_v, w_subln, lam, out_scale, t):
    nb, past, _ = cache_k.shape
    kern = functools.partial(_attn_sample_kernel, t=t, out_scale=out_scale)
    row = lambda b: (b, 0)
    return pl.pallas_call(
        kern,
        grid=(nb,),
        in_specs=[pl.BlockSpec((t, D_MODEL), row), pl.BlockSpec((t, D_MODEL), row),
                  pl.BlockSpec((t, D_MODEL), row),
                  pl.BlockSpec((1, past, D_MODEL), lambda b: (b, 0, 0)),
                  pl.BlockSpec((1, past, D_MODEL), lambda b: (b, 0, 0)),
                  pl.BlockSpec((1, V_DIM), lambda b: (0, 0)),
                  pl.BlockSpec((1, V_DIM), lambda b: (0, 0))],
        out_specs=pl.BlockSpec((t, D_MODEL), row),
        out_shape=jax.ShapeDtypeStruct((nb * t, D_MODEL), BF16),
        compiler_params=_cparams("arbitrary"),
        name="attn_sample",
    )(q, kb, vb, cache_k, cache_v, w_subln, lam)


def _route(x, nw, wr, br):
    h = _rms(x, nw).astype(BF16)
    lg = jnp.dot(h, wr, preferred_element_type=F32) + br
    lane = lax.broadcasted_iota(jnp.int32, lg.shape, 1)
    big = jnp.int32(LANES)
    neg = jnp.float32(-jnp.inf)

    def masked_softmax(mask):
        z = jnp.where(mask, lg, neg)
        e = jnp.exp(z - jnp.max(z, axis=-1, keepdims=True))
        return e / jnp.sum(e, axis=-1, keepdims=True)

    def top1(p, mask):
        v = jnp.max(jnp.where(mask, p, -1.0), axis=-1, keepdims=True)
        idx = jnp.min(jnp.where(mask & (p == v), lane, big), axis=-1, keepdims=True)
        return v, idx

    gmask = lane < N_EXPERT_GROUPS
    g_val, g_idx = top1(masked_softmax(gmask), gmask)
    lo = N_EXPERT_GROUPS + g_idx * EXPERTS_PER_GROUP
    emask = (lane >= lo) & (lane < lo + EXPERTS_PER_GROUP)
    pe = masked_softmax(emask)
    v1, i1 = top1(pe, emask)
    v2, i2 = top1(pe, emask & (lane != i1))
    den = v1 + v2
    w1 = g_val * v1 / den
    w2 = g_val * v2 / den
    comb = (jnp.where(lane == i1 - N_EXPERT_GROUPS, w1, 0.0)
            + jnp.where(lane == i2 - N_EXPERT_GROUPS, w2, 0.0))
    return h, comb


def _wo_kernel(o_ref, x_ref, wo_ref, x1_ref):
    x1_ref[...] = x_ref[...] + jnp.dot(o_ref[...], wo_ref[...], preferred_element_type=F32)


def _wo(o, x, wo, tm):
    n = x.shape[0]
    row = lambda i: (i, 0)
    return pl.pallas_call(
        _wo_kernel,
        grid=(n // tm,),
        in_specs=[pl.BlockSpec((tm, D_MODEL), row), pl.BlockSpec((tm, D_MODEL), row),
                  pl.BlockSpec((D_MODEL, D_MODEL), lambda i: (0, 0))],
        out_specs=pl.BlockSpec((tm, D_MODEL), row),
        out_shape=jax.ShapeDtypeStruct((n, D_MODEL), F32),
        compiler_params=_cparams("arbitrary"),
        name="attn_out_proj",
    )(o, x, wo)


def _moe_kernel(x_ref, nfw_ref, wr_ref, br_ref, wg_ref, wu_ref, wd_ref, nw_ref, y_ref,
                h_ref, c_ref, *, final_norm):
    g = pl.program_id(1)

    @pl.when(g == 0)
    def _():
        x = x_ref[...]
        y_ref[...] = x
        h_ref[...], c_ref[...] = _route(x, nfw_ref[...], wr_ref[...], br_ref[...])

    h = h_ref[...]
    comb = c_ref[...]
    acc = y_ref[...]
    for e in range(EXPERTS_PER_GROUP):
        gate = jnp.dot(h, wg_ref[e], preferred_element_type=F32)
        up = jnp.dot(h, wu_ref[e], preferred_element_type=F32)
        hid = (gate * jax.nn.sigmoid(gate) * up).astype(BF16)
        lane = lax.broadcasted_iota(jnp.int32, comb.shape, 1)
        ce = jnp.sum(jnp.where(lane == g * EXPERTS_PER_GROUP + e, comb, 0.0),
                     axis=-1, keepdims=True)
        acc = acc + ce * jnp.dot(hid, wd_ref[e], preferred_element_type=F32)
    y_ref[...] = acc

    if final_norm:
        @pl.when(g == N_EXPERT_GROUPS - 1)
        def _():
            y_ref[...] = _rms(y_ref[...], nw_ref[...])


def _moe(x, nfw, wr, br, wg, wu, wd, nw, final_norm, tm):
    n = x.shape[0]
    row = lambda i, g: (i, 0)
    fix = lambda i, g: (0, 0)
    kern = functools.partial(_moe_kernel, final_norm=final_norm)
    return pl.pallas_call(
        kern,
        grid=(n // tm, N_EXPERT_GROUPS),
        in_specs=[pl.BlockSpec((tm, D_MODEL), row), pl.BlockSpec((1, D_MODEL), fix),
                  pl.BlockSpec((D_MODEL, LANES), fix), pl.BlockSpec((1, LANES), fix),
                  pl.BlockSpec((EXPERTS_PER_GROUP, D_MODEL, D_EXPERT), lambda i, g: (g, 0, 0)),
                  pl.BlockSpec((EXPERTS_PER_GROUP, D_MODEL, D_EXPERT), lambda i, g: (g, 0, 0)),
                  pl.BlockSpec((EXPERTS_PER_GROUP, D_EXPERT, D_MODEL), lambda i, g: (g, 0, 0)),
                  pl.BlockSpec((1, D_MODEL), lambda i, g: (0, 0))],
        out_specs=pl.BlockSpec((tm, D_MODEL), row),
        out_shape=jax.ShapeDtypeStruct((n, D_MODEL), F32),
        scratch_shapes=[pltpu.VMEM((tm, D_MODEL), BF16), pltpu.VMEM((tm, LANES), F32)],
        compiler_params=_cparams("arbitrary", "arbitrary"),
        name="moe",
    )(x, nfw, wr, br, wg, wu, wd, nw)


def _cmul(ar, ai, br, bi):
    return ar * br - ai * bi, ar * bi + ai * br


def _s5_kernel(x_ref, x0r_ref, x0i_ref, nmw_ref, ar_ref, ai_ref, bm_ref, cm_ref, d_ref,
               wglu_ref, wgate_ref,
               x2_ref, sr_ref, si_ref,
               xp_ref, bu_ref, y_ref, pr_ref, pi_ref, cr_ref, ci_ref, car_r_ref, car_i_ref,
               *, chain):
    steps, rows, nst = SSM_STEPS, SSM_ROWS, SSM_BLK_ST

    @pl.when(pl.program_id(0) == 0)
    def _():
        for kb in range(SSM_NBLK):
            ar, ai = ar_ref[kb], ai_ref[kb]
            pr, pi = ar, ai
            pr_ref[kb, 0:1, :] = pr
            pi_ref[kb, 0:1, :] = pi
            for i in range(1, steps):
                pr, pi = _cmul(ar, ai, pr, pi)
                pr_ref[kb, i:i + 1, :] = pr
                pi_ref[kb, i:i + 1, :] = pi
        car_r_ref[...] = jnp.zeros_like(car_r_ref)
        car_i_ref[...] = jnp.zeros_like(car_i_ref)

    nlb = D_MODEL // LANES
    for i in range(steps):
        for b in range(nlb):
            xp_ref[i * SUBLANES:(i + 1) * SUBLANES, b * LANES:(b + 1) * LANES] = (
                x_ref[pl.ds(i * nlb + b, SUBLANES, stride=steps * nlb), :])

    u = _rms(xp_ref[...], nmw_ref[...])
    ub = u.astype(BF16)

    half = nst // 2
    for kb in range(SSM_NBLK):
        bu_ref[...] = jnp.dot(ub[:, kb * SSM_BLK_CH:(kb + 1) * SSM_BLK_CH], bm_ref[kb],
                              preferred_element_type=F32)
        for hh in range(2):
            re = slice(hh * half, (hh + 1) * half)
            im = slice(nst + hh * half, nst + (hh + 1) * half)
            ar = jnp.broadcast_to(ar_ref[kb, :, re], (SUBLANES, half))
            ai = jnp.broadcast_to(ai_ref[kb, :, re], (SUBLANES, half))

            def scan_step(i, s, re=re, im=im, ar=ar, ai=ai):
                sr, si = s
                r0 = pl.multiple_of(i * SUBLANES, SUBLANES)
                tr, ti = _cmul(ar, ai, sr, si)
                sr = tr + bu_ref[pl.ds(r0, SUBLANES), re]
                si = ti + bu_ref[pl.ds(r0, SUBLANES), im]
                bu_ref[pl.ds(r0, SUBLANES), re] = sr
                bu_ref[pl.ds(r0, SUBLANES), im] = si
                return sr, si

            zero = jnp.zeros((SUBLANES, half), F32)
            lax.fori_loop(0, steps, scan_step, (zero, zero), unroll=True)

        if chain:
            alr, ali = pr_ref[kb, steps - 1:steps, :], pi_ref[kb, steps - 1:steps, :]
            cr, ci = car_r_ref[kb], car_i_ref[kb]
            for j in range(SUBLANES):
                cr_ref[j:j + 1, :] = cr
                ci_ref[j:j + 1, :] = ci
                last = rows - SUBLANES + j
                tr, ti = _cmul(alr, ali, cr, ci)
                cr = tr + bu_ref[last:last + 1, 0:nst]
                ci = ti + bu_ref[last:last + 1, nst:2 * nst]
            car_r_ref[kb] = cr
            car_i_ref[kb] = ci
        else:
            cr_ref[...] = x0r_ref[:, kb * nst:(kb + 1) * nst]
            ci_ref[...] = x0i_ref[:, kb * nst:(kb + 1) * nst]

        c_r, c_i = cr_ref[...], ci_ref[...]

        def fix_step(i, _, kb=kb, c_r=c_r, c_i=c_i):
            r0 = pl.multiple_of(i * SUBLANES, SUBLANES)
            tr, ti = _cmul(pr_ref[kb, pl.ds(i, 1), :], pi_ref[kb, pl.ds(i, 1), :], c_r, c_i)
            bu_ref[pl.ds(r0, SUBLANES), 0:nst] = bu_ref[pl.ds(r0, SUBLANES), 0:nst] + tr
            bu_ref[pl.ds(r0, SUBLANES), nst:2 * nst] = bu_ref[pl.ds(r0, SUBLANES), nst:2 * nst] + ti
            return 0

        lax.fori_loop(0, steps, fix_step, 0, unroll=True)
        sr_ref[:, kb * nst:(kb + 1) * nst] = bu_ref[rows - SUBLANES:rows, 0:nst]
        si_ref[:, kb * nst:(kb + 1) * nst] = bu_ref[rows - SUBLANES:rows, nst:2 * nst]
        y_ref[:, kb * SSM_BLK_CH:(kb + 1) * SSM_BLK_CH] = jnp.dot(
            bu_ref[...].astype(BF16), cm_ref[kb], preferred_element_type=F32)

    y = y_ref[...] + d_ref[...] * u
    z = jax.nn.gelu(y).astype(BF16)
    out = (jnp.dot(z, wglu_ref[...], preferred_element_type=F32)
           * jax.nn.sigmoid(jnp.dot(z, wgate_ref[...], preferred_element_type=F32)))
    y_ref[...] = xp_ref[...] + out
    for i in range(steps):
        for b in range(nlb):
            x2_ref[pl.ds(i * nlb + b, SUBLANES, stride=steps * nlb), :] = (
                y_ref[i * SUBLANES:(i + 1) * SUBLANES, b * LANES:(b + 1) * LANES])


def _s5(x, x0r, x0i, nmw, a_re, a_im, bm, cm, d, wglu, wgate, chain):
    n = x.shape[0]
    nlb = D_MODEL // LANES
    nchunk = n // SSM_ROWS
    row = lambda i: (i, 0)
    fix2 = lambda i: (0, 0)
    fix3 = lambda i: (0, 0, 0)
    state_map = fix2 if chain else row
    n_state_rows = SUBLANES if chain else SUBLANES * nchunk
    kern = functools.partial(_s5_kernel, chain=chain)
    return pl.pallas_call(
        kern,
        grid=(nchunk,),
        in_specs=[pl.BlockSpec((SSM_ROWS * nlb, LANES), row),
                  pl.BlockSpec((SUBLANES, N_STATES), state_map),
                  pl.BlockSpec((SUBLANES, N_STATES), state_map),
                  pl.BlockSpec((1, D_MODEL), fix2),
                  pl.BlockSpec((SSM_NBLK, 1, SSM_BLK_ST), fix3),
                  pl.BlockSpec((SSM_NBLK, 1, SSM_BLK_ST), fix3),
                  pl.BlockSpec((SSM_NBLK, SSM_BLK_CH, 2 * SSM_BLK_ST), fix3),
                  pl.BlockSpec((SSM_NBLK, 2 * SSM_BLK_ST, SSM_BLK_CH), fix3),
                  pl.BlockSpec((1, D_MODEL), fix2),
                  pl.BlockSpec((D_MODEL, D_MODEL), fix2),
                  pl.BlockSpec((D_MODEL, D_MODEL), fix2)],
        out_specs=[pl.BlockSpec((SSM_ROWS * nlb, LANES), row),
                   pl.BlockSpec((SUBLANES, N_STATES), state_map),
                   pl.BlockSpec((SUBLANES, N_STATES), state_map)],
        out_shape=[jax.ShapeDtypeStruct((n * nlb, LANES), F32),
                   jax.ShapeDtypeStruct((n_state_rows, N_STATES), F32),
                   jax.ShapeDtypeStruct((n_state_rows, N_STATES), F32)],
        scratch_shapes=[pltpu.VMEM((SSM_ROWS, D_MODEL), F32),
                        pltpu.VMEM((SSM_ROWS, 2 * SSM_BLK_ST), F32),
                        pltpu.VMEM((SSM_ROWS, D_MODEL), F32),
                        pltpu.VMEM((SSM_NBLK, SSM_STEPS, SSM_BLK_ST), F32),
                        pltpu.VMEM((SSM_NBLK, SSM_STEPS, SSM_BLK_ST), F32),
                        pltpu.VMEM((SUBLANES, SSM_BLK_ST), F32),
                        pltpu.VMEM((SUBLANES, SSM_BLK_ST), F32),
                        pltpu.VMEM((SSM_NBLK, 1, SSM_BLK_ST), F32),
                        pltpu.VMEM((SSM_NBLK, 1, SSM_BLK_ST), F32)],
        compiler_params=_cparams("arbitrary"),
        name="s5_mixer",
    )(x.reshape(n * nlb, LANES), x0r, x0i, nmw, a_re, a_im, bm, cm, d, wglu, wgate)


def _rope_tables(pos):
    half = ROT_DIM // 2
    inv_freq = ROPE_THETA ** (-jnp.arange(half, dtype=F32) * 2.0 / ROT_DIM)
    ang = pos.astype(F32)[:, None] * inv_freq[None, :]
    cos, sin = jnp.cos(ang), jnp.sin(ang)
    n = pos.shape[0]
    pad = HEAD_DIM - ROT_DIM
    cosf = jnp.concatenate([cos, cos, jnp.ones((n, pad), F32)], axis=1)
    sina = jnp.concatenate([-sin, jnp.zeros((n, half + pad), F32)], axis=1)
    sinb = jnp.concatenate([jnp.zeros((n, half), F32), sin, jnp.zeros((n, pad), F32)], axis=1)
    tile2 = lambda t: jnp.concatenate([t, t], axis=1)
    return tile2(cosf), tile2(sina), tile2(sinb), cos.T, sin.T


def _s5_params(lam_re, lam_im, log_dt, b_re, b_im, c_re, c_im):
    dt = jnp.exp(log_dt)[:, None]
    z_re, z_im = lam_re * dt, lam_im * dt
    mag = jnp.exp(z_re)
    lb_re, lb_im = mag * jnp.cos(z_im), mag * jnp.sin(z_im)
    n_re, n_im = lb_re - 1.0, lb_im
    den = lam_re * lam_re + lam_im * lam_im
    k_re = (n_re * lam_re + n_im * lam_im) / den
    k_im = (n_im * lam_re - n_re * lam_im) / den
    bb_re = k_re[..., None] * b_re - k_im[..., None] * b_im
    bb_im = k_re[..., None] * b_im + k_im[..., None] * b_re
    gpb = SSM_BLK_CH // SSM_GROUP
    eye = jnp.eye(gpb, dtype=F32)

    def in_blocks(bb):
        v = bb.reshape(SSM_NBLK, gpb, SSM_STATE, SSM_GROUP).transpose(0, 1, 3, 2)
        return jnp.einsum('kgcp,gh->kgchp', v, eye).reshape(SSM_NBLK, SSM_BLK_CH, SSM_BLK_ST)

    def out_blocks(cc):
        v = cc.reshape(SSM_NBLK, gpb, SSM_GROUP, SSM_STATE).transpose(0, 1, 3, 2)
        return jnp.einsum('kgpc,gh->kgphc', v, eye).reshape(SSM_NBLK, SSM_BLK_ST, SSM_BLK_CH)

    bm = jnp.concatenate([in_blocks(bb_re), in_blocks(bb_im)], axis=2).astype(BF16)
    cm = jnp.concatenate([out_blocks(c_re), -out_blocks(c_im)], axis=1).astype(BF16)
    a_re = lb_re.reshape(SSM_NBLK, 1, SSM_BLK_ST)
    a_im = lb_im.reshape(SSM_NBLK, 1, SSM_BLK_ST)
    return a_re, a_im, bm, cm


def _router_params(w_rg, b_rg, w_re, b_re):
    pad = LANES - N_EXPERT_GROUPS - N_EXPERTS
    wr = jnp.concatenate([w_rg, w_re, jnp.zeros((D_MODEL, pad), F32)], axis=1).astype(BF16)
    br = jnp.concatenate([b_rg, b_re, jnp.zeros((pad,), F32)])[None, :]
    return wr, br


def kernel(x_prompt, x_sample, cache_k, cache_v, state_ssm_re, state_ssm_im, norm_mix, norm_ffn, norm_final, attn_w_qkv, attn_lambda_q1, attn_lambda_k1, attn_lambda_q2, attn_lambda_k2, attn_subln, attn_w_o, ssm_lambda_re, ssm_lambda_im, ssm_log_dt, ssm_b_re, ssm_b_im, ssm_c_re, ssm_c_im, ssm_d, ssm_w_glu, ssm_w_gate, moe_w_router_group, moe_b_router_group, moe_w_router_expert, moe_b_router_expert, moe_w_gate, moe_w_up, moe_w_down):
    _, seq, _ = x_prompt.shape
    nb, dec, _ = x_sample.shape
    past = cache_k.shape[2]
    xp = x_prompt.reshape(seq, D_MODEL)
    xs = x_sample.reshape(nb * dec, D_MODEL)

    lam_init = 0.8 - 0.6 * math.exp(-0.3 * 0)
    lam = (jnp.exp(jnp.sum(attn_lambda_q1[0] * attn_lambda_k1[0]))
           - jnp.exp(jnp.sum(attn_lambda_q2[0] * attn_lambda_k2[0])) + lam_init)
    lam_p = jnp.full((1, ATT_T), lam, F32)
    lam_s = jnp.full((1, V_DIM), lam, F32)
    w_subln = attn_subln[0][None, :]
    out_scale = 1.0 - lam_init
    wqkv = attn_w_qkv[0].astype(BF16)
    wq, wk, wv = (wqkv[:, j * D_MODEL:(j + 1) * D_MODEL] for j in range(3))
    wo = attn_w_o[0].astype(BF16)
    nm0 = norm_mix[0][None, :]
    tabs_p = _rope_tables(jnp.arange(seq, dtype=jnp.int32))
    tabs_s = tuple(jnp.tile(t, (nb, 1))
                   for t in _rope_tables(past + jnp.arange(dec, dtype=jnp.int32))[:3])

    k_p, v_p, kb_p, qt_p, vt_p = _qkv_prompt(xp, nm0, wq.T, wk, wv, wv.T, *tabs_p, tm=512)
    q_s, k_s, v_s, kb_s, vb_s = _qkv(xs, nm0, wqkv, *tabs_s, tm=512)
    o_p = _attn_prompt(qt_p, kb_p, vt_p, w_subln, lam_p, out_scale)
    o_s = _attn_sample(q_s, kb_s, vb_s, cache_k[0].reshape(nb, past, D_MODEL),
                       cache_v[0].reshape(nb, past, D_MODEL), w_subln, lam_s, out_scale, t=dec)

    nf0 = norm_ffn[0][None, :]
    wr0, br0 = _router_params(moe_w_router_group[0], moe_b_router_group[0],
                              moe_w_router_expert[0], moe_b_router_expert[0])
    x1_p = _wo(o_p, xp, wo, tm=512)
    x1_s = _wo(o_s, xs, wo, tm=512)
    wg0, wu0, wd0 = (moe_w_gate[0].astype(BF16), moe_w_up[0].astype(BF16),
                     moe_w_down[0].astype(BF16))
    nfin = norm_final[None, :]
    xp1 = _moe(x1_p, nf0, wr0, br0, wg0, wu0, wd0, nfin, False, tm=1024)
    xs1 = _moe(x1_s, nf0, wr0, br0, wg0, wu0, wd0, nfin, False, tm=512)

    a_re, a_im, bm, cm = _s5_params(ssm_lambda_re[0], ssm_lambda_im[0], ssm_log_dt[0],
                                    ssm_b_re[0], ssm_b_im[0], ssm_c_re[0], ssm_c_im[0])
    nm1 = norm_mix[1][None, :]
    nf1 = norm_ffn[1][None, :]
    wr1, br1 = _router_params(moe_w_router_group[1], moe_b_router_group[1],
                              moe_w_router_expert[1], moe_b_router_expert[1])
    s5w = (nm1, a_re, a_im, bm, cm, ssm_d[0][None, :], ssm_w_glu[0].astype(BF16),
           ssm_w_gate[0].astype(BF16))
    zero_state = jnp.zeros((SUBLANES, N_STATES), F32)
    x2_p, sr_p, si_p = _s5(xp1, zero_state, zero_state, *s5w, chain=True)
    x0r = state_ssm_re[0].reshape(nb, N_STATES)
    x0i = state_ssm_im[0].reshape(nb, N_STATES)
    x2_s, sr_s, si_s = _s5(xs1, x0r, x0i, *s5w, chain=False)

    wg1, wu1, wd1 = (moe_w_gate[1].astype(BF16), moe_w_up[1].astype(BF16),
                     moe_w_down[1].astype(BF16))
    y_p = _moe(x2_p.reshape(seq, D_MODEL), nf1, wr1, br1, wg1, wu1, wd1, nfin, True, tm=1024)
    y_s = _moe(x2_s.reshape(nb * dec, D_MODEL), nf1, wr1, br1, wg1, wu1, wd1, nfin, True, tm=512)

    hshape = (N_HEADS, V_DIM)
    gshape = (N_SSM_GROUPS, SSM_STATE)
    return (y_p.reshape(1, seq, D_MODEL), y_s.reshape(nb, dec, D_MODEL),
            k_p.reshape(1, 1, seq, *hshape), v_p.reshape(1, 1, seq, *hshape),
            sr_p[SUBLANES - 1].reshape(1, 1, *gshape), si_p[SUBLANES - 1].reshape(1, 1, *gshape),
            k_s.reshape(1, nb, dec, *hshape), v_s.reshape(1, nb, dec, *hshape),
            sr_s.reshape(1, nb, *gshape), si_s.reshape(1, nb, *gshape))
```

```python
import functools
import math

import jax
import jax.numpy as jnp
from jax import lax
from jax.experimental import pallas as pl
from jax.experimental.pallas import tpu as pltpu

F32 = jnp.float32
BF16 = jnp.bfloat16

D_MODEL = 1024
CHUNK = 64
N_HEADS = 8
HEAD_DIM = 64
V_DIM = 2 * HEAD_DIM
ROT_DIM = HEAD_DIM // 4
ROPE_THETA = 500000.0
SSM_GROUP = 16
N_SSM_GROUPS = D_MODEL // SSM_GROUP
SSM_STATE = 64
N_STATES = N_SSM_GROUPS * SSM_STATE
N_EXPERT_GROUPS = 4
EXPERTS_PER_GROUP = 4
N_EXPERTS = N_EXPERT_GROUPS * EXPERTS_PER_GROUP
D_EXPERT = D_MODEL // 4
RMS_EPS = 1e-6

LANES = 128
SUBLANES = 8
VMEM_LIMIT = 56 * 1024 * 1024

SSM_BLK_CH = 256
SSM_NBLK = D_MODEL // SSM_BLK_CH
SSM_BLK_ST = (SSM_BLK_CH // SSM_GROUP) * SSM_STATE
SSM_STEPS = 32
SSM_ROWS = SUBLANES * SSM_STEPS


def _cparams(*sem):
    return pltpu.CompilerParams(dimension_semantics=sem, vmem_limit_bytes=VMEM_LIMIT)


def _rms(x, w):
    return x * lax.rsqrt(jnp.mean(x * x, axis=-1, keepdims=True) + RMS_EPS) * w


def _qkv_kernel(x_ref, nw_ref, w_ref, cos_ref, sina_ref, sinb_ref,
                q_ref, k_ref, v_ref, kb_ref, vb_ref):
    h = _rms(x_ref[...], nw_ref[...]).astype(BF16)
    cosf, sina, sinb = cos_ref[...], sina_ref[...], sinb_ref[...]

    def rope(t):
        outs = []
        for hh in range(N_HEADS):
            xs = t[:, hh * LANES:(hh + 1) * LANES]
            outs.append(xs * cosf + pltpu.roll(xs, LANES - ROT_DIM // 2, 1) * sina
                        + pltpu.roll(xs, ROT_DIM // 2, 1) * sinb)
        return jnp.concatenate(outs, axis=1)

    q = rope(jnp.dot(h, w_ref[:, 0:D_MODEL], preferred_element_type=F32))
    q_ref[...] = (q * (HEAD_DIM ** -0.5)).astype(BF16)
    k = rope(jnp.dot(h, w_ref[:, D_MODEL:2 * D_MODEL], preferred_element_type=F32))
    k_ref[...] = k
    kb_ref[...] = k.astype(BF16)
    v = jnp.dot(h, w_ref[:, 2 * D_MODEL:3 * D_MODEL], preferred_element_type=F32)
    v_ref[...] = v
    vb_ref[...] = v.astype(BF16)


def _qkv(x, nw, w, cosf, sina, sinb, tm):
    n = x.shape[0]
    row = lambda i: (i, 0)
    fix = lambda i: (0, 0)
    return pl.pallas_call(
        _qkv_kernel,
        grid=(n // tm,),
        in_specs=[pl.BlockSpec((tm, D_MODEL), row), pl.BlockSpec((1, D_MODEL), fix),
                  pl.BlockSpec((D_MODEL, 3 * D_MODEL), fix),
                  pl.BlockSpec((tm, LANES), row), pl.BlockSpec((tm, LANES), row),
                  pl.BlockSpec((tm, LANES), row)],
        out_specs=[pl.BlockSpec((tm, D_MODEL), row)] * 5,
        out_shape=[jax.ShapeDtypeStruct((n, D_MODEL), BF16),
                   jax.ShapeDtypeStruct((n, D_MODEL), F32),
                   jax.ShapeDtypeStruct((n, D_MODEL), F32),
                   jax.ShapeDtypeStruct((n, D_MODEL), BF16),
                   jax.ShapeDtypeStruct((n, D_MODEL), BF16)],
        compiler_params=_cparams("arbitrary"),
        name="qkv_rope",
    )(x, nw, w, cosf, sina, sinb)


ATT_T = 512
ATT_K = 256
ATT_ONES = 16
ATT_UNROLL = 4


def _qkv_prompt_kernel(x_ref, nw_ref, wqt_ref, wk_ref, wv_ref, wvt_ref,
                       cos_ref, sina_ref, sinb_ref, cost_ref, sint_ref,
                       k_ref, v_ref, kb_ref, qt_ref, vt_ref):
    h = _rms(x_ref[...], nw_ref[...]).astype(BF16)
    cosf, sina, sinb = cos_ref[...], sina_ref[...], sinb_ref[...]
    outs = []
    k = jnp.dot(h, wk_ref[...], preferred_element_type=F32)
    for hh in range(N_HEADS):
        xs = k[:, hh * LANES:(hh + 1) * LANES]
        outs.append(xs * cosf + pltpu.roll(xs, LANES - ROT_DIM // 2, 1) * sina
                    + pltpu.roll(xs, ROT_DIM // 2, 1) * sinb)
    k = jnp.concatenate(outs, axis=1)
    k_ref[...] = k
    kb_ref[...] = k.astype(BF16)
    v_ref[...] = jnp.dot(h, wv_ref[...], preferred_element_type=F32)

    nt = (((1,), (1,)), ((), ()))
    vt = lax.dot_general(wvt_ref[...], h, nt, preferred_element_type=F32).astype(BF16)
    qt = lax.dot_general(wqt_ref[...], h, nt, preferred_element_type=F32) * (
        HEAD_DIM ** -0.5 * math.log2(math.e))
    cost, sint = cost_ref[...], sint_ref[...]
    half = ROT_DIM // 2
    pieces = []
    for blk in range(D_MODEL // HEAD_DIM):
        r0 = blk * HEAD_DIM
        x1, x2 = qt[r0:r0 + half], qt[r0 + half:r0 + ROT_DIM]
        pieces += [x1 * cost - x2 * sint, x2 * cost + x1 * sint, qt[r0 + ROT_DIM:r0 + HEAD_DIM]]
    qt = jnp.concatenate(pieces, axis=0).astype(BF16)
    for b in range(qt_ref.shape[0]):
        qt_ref[b] = qt[:, b * ATT_T:(b + 1) * ATT_T]
    for b in range(vt_ref.shape[0]):
        vt_ref[b] = vt[:, b * ATT_K:(b + 1) * ATT_K]


def _qkv_prompt(x, nw, wqt, wk, wv, wvt, cosf, sina, sinb, cost, sint, tm):
    n = x.shape[0]
    row = lambda i: (i, 0)
    fix = lambda i: (0, 0)
    wspec = pl.BlockSpec((D_MODEL, D_MODEL), fix)
    tspec = pl.BlockSpec((tm, LANES), row)
    ttspec = pl.BlockSpec((ROT_DIM // 2, tm), lambda i: (0, i))
    qblk = pl.BlockSpec((tm // ATT_T, D_MODEL, ATT_T), lambda i: (i, 0, 0))
    vblk = pl.BlockSpec((tm // ATT_K, D_MODEL, ATT_K), lambda i: (i, 0, 0))
    return pl.pallas_call(
        _qkv_prompt_kernel,
        grid=(n // tm,),
        in_specs=[pl.BlockSpec((tm, D_MODEL), row), pl.BlockSpec((1, D_MODEL), fix),
                  wspec, wspec, wspec, wspec, tspec, tspec, tspec, ttspec, ttspec],
        out_specs=[pl.BlockSpec((tm, D_MODEL), row)] * 3 + [qblk, vblk],
        out_shape=[jax.ShapeDtypeStruct((n, D_MODEL), F32),
                   jax.ShapeDtypeStruct((n, D_MODEL), F32),
                   jax.ShapeDtypeStruct((n, D_MODEL), BF16),
                   jax.ShapeDtypeStruct((n // ATT_T, D_MODEL, ATT_T), BF16),
                   jax.ShapeDtypeStruct((n // ATT_K, D_MODEL, ATT_K), BF16)],
        compiler_params=_cparams("arbitrary"),
        name="qkv_rope_prompt",
    )(x, nw, wqt, wk, wv, wvt, cosf, sina, sinb, cost, sint)


def _stack_subheads(q):
    lane = lax.broadcasted_iota(jnp.int32, q.shape, 1)
    zero = jnp.zeros_like(q)
    return jnp.concatenate([jnp.where(lane < HEAD_DIM, q, zero),
                            jnp.where(lane >= HEAD_DIM, q, zero)], axis=0)


def _scores(qs, k):
    return lax.dot_general(qs, k, (((1,), (1,)), ((), ())), preferred_element_type=F32)


def _diff_finish(acc, l, t, lam, w, out_scale):
    o = acc[:t] / l[:t] - lam * (acc[t:] / l[t:])
    return _rms(o, w) * out_scale


def _attn_prompt_kernel(qt_ref, k_ref, vt_ref, w_ref, lam_ref, o_ref, sa_ref, sb_ref, acc_ref,
                        *, out_scale):
    t, tk = ATT_T, ATT_K
    i = pl.program_id(1)
    qt = qt_ref[0]
    row = lax.broadcasted_iota(jnp.int32, qt.shape, 0)
    zero = jnp.zeros_like(qt)
    qs = jnp.concatenate([jnp.where(row < HEAD_DIM, qt, zero),
                          jnp.where(row >= HEAD_DIM, qt, zero)], axis=1)

    def scores(j):
        r0 = pl.multiple_of(j * tk, tk)
        return jnp.dot(k_ref[pl.ds(r0, tk), :], qs, preferred_element_type=F32)

    ones = jnp.ones((ATT_ONES, tk), BF16)

    def update(m, s, j):
        m_new = jnp.maximum(m, jnp.max(s, axis=0, keepdims=True))
        alpha = jnp.exp2(m - m_new)
        p = jnp.exp2(s - m_new).astype(BF16)
        v1 = jnp.concatenate([vt_ref[j], ones], axis=0)
        acc_ref[...] = alpha * acc_ref[...] + jnp.dot(v1, p, preferred_element_type=F32)
        return m_new

    def masked(s, j):
        kc = (j * tk + lax.broadcasted_iota(jnp.int32, (tk, 2 * t), 0)) // CHUNK
        qc = (i * t + lax.broadcasted_iota(jnp.int32, (tk, 2 * t), 1) % t) // CHUNK
        return jnp.where(kc <= qc, s, -jnp.inf)

    acc_ref[...] = jnp.zeros_like(acc_ref)
    sa_ref[...] = scores(0)

    def pairs(n):
        def body(jj, m):
            for u in range(n):
                j = 2 * (jj * n + u)
                sb_ref[...] = scores(j + 1)
                m = update(m, sa_ref[...], j)
                sa_ref[...] = scores(j + 2)
                m = update(m, sb_ref[...], j + 1)
            return m
        return body

    m = jnp.full((1, 2 * t), -jnp.inf, F32)
    nlong = i // ATT_UNROLL
    m = lax.fori_loop(0, nlong, pairs(ATT_UNROLL), m)
    m = lax.fori_loop(nlong * ATT_UNROLL, i, pairs(1), m)

    sb_ref[...] = scores(2 * i + 1)
    m = update(m, masked(sa_ref[...], 2 * i), 2 * i)
    update(m, masked(sb_ref[...], 2 * i + 1), 2 * i + 1)

    acc = acc_ref[0:V_DIM, :]
    l = acc_ref[V_DIM:V_DIM + 1, :]
    o = acc[:, :t] / l[:, :t] - lam_ref[...] * (acc[:, t:] / l[:, t:])
    o = o * lax.rsqrt(jnp.mean(o * o, axis=0, keepdims=True) + RMS_EPS)
    o_ref[...] = (o.T * w_ref[...] * out_scale).astype(BF16)


def _attn_prompt(qt, kb, vt, w_subln, lam, out_scale):
    l = kb.shape[0]
    t, tk = ATT_T, ATT_K
    kern = functools.partial(_attn_prompt_kernel, out_scale=out_scale)
    return pl.pallas_call(
        kern,
        grid=(N_HEADS, l // t),
        in_specs=[pl.BlockSpec((1, V_DIM, t), lambda h, i: (i, h, 0)),
                  pl.BlockSpec((l, V_DIM), lambda h, i: (0, h)),
                  pl.BlockSpec((l // tk, V_DIM, tk), lambda h, i: (0, h, 0)),
                  pl.BlockSpec((1, V_DIM), lambda h, i: (0, 0)),
                  pl.BlockSpec((1, t), lambda h, i: (0, 0))],
        out_specs=pl.BlockSpec((t, V_DIM), lambda h, i: (i, h)),
        out_shape=jax.ShapeDtypeStruct((l, D_MODEL), BF16),
        scratch_shapes=[pltpu.VMEM((tk, 2 * t), F32), pltpu.VMEM((tk, 2 * t), F32),
                        pltpu.VMEM((V_DIM + ATT_ONES, 2 * t), F32)],
        compiler_params=_cparams("arbitrary", "arbitrary"),
        name="attn_prompt",
    )(qt, kb, vt, w_subln, lam)


def _attn_sample_kernel(q_ref, kn_ref, vn_ref, ck_ref, cv_ref, w_ref, lam_ref, o_ref,
                        *, t, out_scale):
    for hh in range(N_HEADS):
        cols = slice(hh * V_DIM, (hh + 1) * V_DIM)
        qs = _stack_subheads(q_ref[:, cols])
        kc = ck_ref[0, :, cols].astype(BF16)
        vc = cv_ref[0, :, cols].astype(BF16)
        s_c = _scores(qs, kc)
        s_n = _scores(qs, kn_ref[:, cols])
        m = jnp.maximum(jnp.max(s_c, axis=-1, keepdims=True), jnp.max(s_n, axis=-1, keepdims=True))
        p_c = jnp.exp(s_c - m)
        p_n = jnp.exp(s_n - m)
        l = jnp.sum(p_c, axis=-1, keepdims=True) + jnp.sum(p_n, axis=-1, keepdims=True)
        acc = (jnp.dot(p_c.astype(BF16), vc, preferred_element_type=F32)
               + jnp.dot(p_n.astype(BF16), vn_ref[:, cols], preferred_element_type=F32))
        o_ref[:, cols] = _diff_finish(acc, l, t, lam_ref[...], w_ref[...], out_scale).astype(BF16)


def _attn_sample(q, kb, vb, cache_k, cache_v, w_subln, lam, out_scale, t):
    nb, past, _ = cache_k.shape
    kern = functools.partial(_attn_sample_kernel, t=t, out_scale=out_scale)
    row = lambda b: (b, 0)
    return pl.pallas_call(
        kern,
        grid=(nb,),
        in_specs=[pl.BlockSpec((t, D_MODEL), row), pl.BlockSpec((t, D_MODEL), row),
                  pl.BlockSpec((t, D_MODEL), row),
                  pl.BlockSpec((1, past, D_MODEL), lambda b: (b, 0, 0)),
                  pl.BlockSpec((1, past, D_MODEL), lambda b: (b, 0, 0)),
                  pl.BlockSpec((1, V_DIM), lambda b: (0, 0)),
                  pl.BlockSpec((1, V_DIM), lambda b: (0, 0))],
        out_specs=pl.BlockSpec((t, D_MODEL), row),
        out_shape=jax.ShapeDtypeStruct((nb * t, D_MODEL), BF16),
        compiler_params=_cparams("arbitrary"),
        name="attn_sample",
    )(q, kb, vb, cache_k, cache_v, w_subln, lam)


def _route(x, nw, wr, br):
    hf = _rms(x, nw)
    h = hf.astype(BF16)
    lg = jnp.dot(hf, wr, precision=lax.Precision.HIGHEST,
                 preferred_element_type=F32) + br
    lane = lax.broadcasted_iota(jnp.int32, lg.shape, 1)
    big = jnp.int32(LANES)
    neg = jnp.float32(-jnp.inf)

    def masked_softmax(mask):
        z = jnp.where(mask, lg, neg)
        e = jnp.exp(z - jnp.max(z, axis=-1, keepdims=True))
        return e / jnp.sum(e, axis=-1, keepdims=True)

    def top1(p, mask):
        v = jnp.max(jnp.where(mask, p, -1.0), axis=-1, keepdims=True)
        idx = jnp.min(jnp.where(mask & (p == v), lane, big), axis=-1, keepdims=True)
        return v, idx

    gmask = lane < N_EXPERT_GROUPS
    g_val, g_idx = top1(masked_softmax(gmask), gmask)
    lo = N_EXPERT_GROUPS + g_idx * EXPERTS_PER_GROUP
    emask = (lane >= lo) & (lane < lo + EXPERTS_PER_GROUP)
    pe = masked_softmax(emask)
    v1, i1 = top1(pe, emask)
    v2, i2 = top1(pe, emask & (lane != i1))
    den = v1 + v2
    w1 = g_val * v1 / den
    w2 = g_val * v2 / den
    comb = (jnp.where(lane == i1 - N_EXPERT_GROUPS, w1, 0.0)
            + jnp.where(lane == i2 - N_EXPERT_GROUPS, w2, 0.0))
    return h, comb


def _wo_kernel(o_ref, x_ref, wo_ref, x1_ref):
    x1_ref[...] = x_ref[...] + jnp.dot(o_ref[...], wo_ref[...], preferred_element_type=F32)


def _wo(o, x, wo, tm):
    n = x.shape[0]
    row = lambda i: (i, 0)
    return pl.pallas_call(
        _wo_kernel,
        grid=(n // tm,),
        in_specs=[pl.BlockSpec((tm, D_MODEL), row), pl.BlockSpec((tm, D_MODEL), row),
                  pl.BlockSpec((D_MODEL, D_MODEL), lambda i: (0, 0))],
        out_specs=pl.BlockSpec((tm, D_MODEL), row),
        out_shape=jax.ShapeDtypeStruct((n, D_MODEL), F32),
        compiler_params=_cparams("arbitrary"),
        name="attn_out_proj",
    )(o, x, wo)


def _moe_kernel(x_ref, nfw_ref, wr_ref, br_ref, wg_ref, wu_ref, wd_ref, nw_ref, y_ref,
                h_ref, c_ref, *, final_norm):
    g = pl.program_id(1)

    @pl.when(g == 0)
    def _():
        x = x_ref[...]
        y_ref[...] = x
        h_ref[...], c_ref[...] = _route(x, nfw_ref[...], wr_ref[...], br_ref[...])

    h = h_ref[...]
    comb = c_ref[...]
    acc = y_ref[...]
    for e in range(EXPERTS_PER_GROUP):
        gate = jnp.dot(h, wg_ref[e], preferred_element_type=F32)
        up = jnp.dot(h, wu_ref[e], preferred_element_type=F32)
        hid = (gate * jax.nn.sigmoid(gate) * up).astype(BF16)
        lane = lax.broadcasted_iota(jnp.int32, comb.shape, 1)
        ce = jnp.sum(jnp.where(lane == g * EXPERTS_PER_GROUP + e, comb, 0.0),
                     axis=-1, keepdims=True)
        acc = acc + ce * jnp.dot(hid, wd_ref[e], preferred_element_type=F32)
    y_ref[...] = acc

    if final_norm:
        @pl.when(g == N_EXPERT_GROUPS - 1)
        def _():
            y_ref[...] = _rms(y_ref[...], nw_ref[...])


def _moe(x, nfw, wr, br, wg, wu, wd, nw, final_norm, tm):
    n = x.shape[0]
    row = lambda i, g: (i, 0)
    fix = lambda i, g: (0, 0)
    kern = functools.partial(_moe_kernel, final_norm=final_norm)
    return pl.pallas_call(
        kern,
        grid=(n // tm, N_EXPERT_GROUPS),
        in_specs=[pl.BlockSpec((tm, D_MODEL), row), pl.BlockSpec((1, D_MODEL), fix),
                  pl.BlockSpec((D_MODEL, LANES), fix), pl.BlockSpec((1, LANES), fix),
                  pl.BlockSpec((EXPERTS_PER_GROUP, D_MODEL, D_EXPERT), lambda i, g: (g, 0, 0)),
                  pl.BlockSpec((EXPERTS_PER_GROUP, D_MODEL, D_EXPERT), lambda i, g: (g, 0, 0)),
                  pl.BlockSpec((EXPERTS_PER_GROUP, D_EXPERT, D_MODEL), lambda i, g: (g, 0, 0)),
                  pl.BlockSpec((1, D_MODEL), lambda i, g: (0, 0))],
        out_specs=pl.BlockSpec((tm, D_MODEL), row),
        out_shape=jax.ShapeDtypeStruct((n, D_MODEL), F32),
        scratch_shapes=[pltpu.VMEM((tm, D_MODEL), BF16), pltpu.VMEM((tm, LANES), F32)],
        compiler_params=_cparams("arbitrary", "arbitrary"),
        name="moe",
    )(x, nfw, wr, br, wg, wu, wd, nw)


def _cmul(ar, ai, br, bi):
    return ar * br - ai * bi, ar * bi + ai * br


def _s5_kernel(x_ref, x0r_ref, x0i_ref, nmw_ref, ar_ref, ai_ref, bm_ref, cm_ref, d_ref,
               wglu_ref, wgate_ref,
               x2_ref, sr_ref, si_ref,
               xp_ref, bu_all_ref, y_ref, pr_ref, pi_ref, cr_ref, ci_ref, car_r_ref, car_i_ref,
               *, chain):
    steps, rows, nst = SSM_STEPS, SSM_ROWS, SSM_BLK_ST

    @pl.when(pl.program_id(0) == 0)
    def _():
        for kb in range(SSM_NBLK):
            ar, ai = ar_ref[kb], ai_ref[kb]
            pr, pi = ar, ai
            for i in range(steps):
                if i:
                    pr, pi = _cmul(ar, ai, pr, pi)
                pr_ref[kb, i * SUBLANES:(i + 1) * SUBLANES, :] = jnp.broadcast_to(pr, (SUBLANES, nst))
                pi_ref[kb, i * SUBLANES:(i + 1) * SUBLANES, :] = jnp.broadcast_to(pi, (SUBLANES, nst))
        car_r_ref[...] = jnp.zeros_like(car_r_ref)
        car_i_ref[...] = jnp.zeros_like(car_i_ref)

    nlb = D_MODEL // LANES
    for i in range(steps):
        for b in range(nlb):
            xp_ref[i * SUBLANES:(i + 1) * SUBLANES, b * LANES:(b + 1) * LANES] = (
                x_ref[pl.ds(i * nlb + b, SUBLANES, stride=steps * nlb), :])

    u = _rms(xp_ref[...], nmw_ref[...])
    ub = u.astype(BF16)

    half = nst // 2
    for kb in range(SSM_NBLK):
        bu_ref = bu_all_ref.at[kb % bu_all_ref.shape[0]]
        bu_ref[...] = jnp.dot(ub[:, kb * SSM_BLK_CH:(kb + 1) * SSM_BLK_CH], bm_ref[kb],
                              preferred_element_type=F32)
        for hh in range(2):
            re = slice(hh * half, (hh + 1) * half)
            im = slice(nst + hh * half, nst + (hh + 1) * half)
            ar = jnp.broadcast_to(ar_ref[kb, :, re], (SUBLANES, half))
            ai = jnp.broadcast_to(ai_ref[kb, :, re], (SUBLANES, half))

            def scan_step(i, s, re=re, im=im, ar=ar, ai=ai, bu_ref=bu_ref):
                sr, si = s
                r0 = pl.multiple_of(i * SUBLANES, SUBLANES)
                tr, ti = _cmul(ar, ai, sr, si)
                sr = tr + bu_ref[pl.ds(r0, SUBLANES), re]
                si = ti + bu_ref[pl.ds(r0, SUBLANES), im]
                bu_ref[pl.ds(r0, SUBLANES), re] = sr
                bu_ref[pl.ds(r0, SUBLANES), im] = si
                return sr, si

            zero = jnp.zeros((SUBLANES, half), F32)
            lax.fori_loop(0, steps, scan_step, (zero, zero), unroll=True)

        if chain:
            alr, ali = pr_ref[kb, rows - 1:rows, :], pi_ref[kb, rows - 1:rows, :]
            cr, ci = car_r_ref[kb], car_i_ref[kb]
            for j in range(SUBLANES):
                cr_ref[j:j + 1, :] = cr
                ci_ref[j:j + 1, :] = ci
                last = rows - SUBLANES + j
                tr, ti = _cmul(alr, ali, cr, ci)
                cr = tr + bu_ref[last:last + 1, 0:nst]
                ci = ti + bu_ref[last:last + 1, nst:2 * nst]
            car_r_ref[kb] = cr
            car_i_ref[kb] = ci
        else:
            cr_ref[...] = x0r_ref[:, kb * nst:(kb + 1) * nst]
            ci_ref[...] = x0i_ref[:, kb * nst:(kb + 1) * nst]

        c_r, c_i = cr_ref[...], ci_ref[...]

        def fix_step(i, _, kb=kb, c_r=c_r, c_i=c_i, bu_ref=bu_ref):
            r0 = pl.multiple_of(i * SUBLANES, SUBLANES)
            tr, ti = _cmul(pr_ref[kb, pl.ds(r0, SUBLANES), :], pi_ref[kb, pl.ds(r0, SUBLANES), :],
                           c_r, c_i)
            bu_ref[pl.ds(r0, SUBLANES), 0:nst] = bu_ref[pl.ds(r0, SUBLANES), 0:nst] + tr
            bu_ref[pl.ds(r0, SUBLANES), nst:2 * nst] = bu_ref[pl.ds(r0, SUBLANES), nst:2 * nst] + ti
            return 0

        lax.fori_loop(0, steps, fix_step, 0, unroll=True)
        sr_ref[:, kb * nst:(kb + 1) * nst] = bu_ref[rows - SUBLANES:rows, 0:nst]
        si_ref[:, kb * nst:(kb + 1) * nst] = bu_ref[rows - SUBLANES:rows, nst:2 * nst]
        y_ref[:, kb * SSM_BLK_CH:(kb + 1) * SSM_BLK_CH] = jnp.dot(
            bu_ref[...].astype(BF16), cm_ref[kb], preferred_element_type=F32)

    y = y_ref[...] + d_ref[...] * u
    z = jax.nn.gelu(y).astype(BF16)
    out = (jnp.dot(z, wglu_ref[...], preferred_element_type=F32)
           * jax.nn.sigmoid(jnp.dot(z, wgate_ref[...], preferred_element_type=F32)))
    y_ref[...] = xp_ref[...] + out
    for i in range(steps):
        for b in range(nlb):
            x2_ref[pl.ds(i * nlb + b, SUBLANES, stride=steps * nlb), :] = (
                y_ref[i * SUBLANES:(i + 1) * SUBLANES, b * LANES:(b + 1) * LANES])


def _s5(x, x0r, x0i, nmw, a_re, a_im, bm, cm, d, wglu, wgate, chain):
    n = x.shape[0]
    nlb = D_MODEL // LANES
    nchunk = n // SSM_ROWS
    row = lambda i: (i, 0)
    fix2 = lambda i: (0, 0)
    fix3 = lambda i: (0, 0, 0)
    state_map = fix2 if chain else row
    n_state_rows = SUBLANES if chain else SUBLANES * nchunk
    kern = functools.partial(_s5_kernel, chain=chain)
    return pl.pallas_call(
        kern,
        grid=(nchunk,),
        in_specs=[pl.BlockSpec((SSM_ROWS * nlb, LANES), row),
                  pl.BlockSpec((SUBLANES, N_STATES), state_map),
                  pl.BlockSpec((SUBLANES, N_STATES), state_map),
                  pl.BlockSpec((1, D_MODEL), fix2),
                  pl.BlockSpec((SSM_NBLK, 1, SSM_BLK_ST), fix3),
                  pl.BlockSpec((SSM_NBLK, 1, SSM_BLK_ST), fix3),
                  pl.BlockSpec((SSM_NBLK, SSM_BLK_CH, 2 * SSM_BLK_ST), fix3),
                  pl.BlockSpec((SSM_NBLK, 2 * SSM_BLK_ST, SSM_BLK_CH), fix3),
                  pl.BlockSpec((1, D_MODEL), fix2),
                  pl.BlockSpec((D_MODEL, D_MODEL), fix2),
                  pl.BlockSpec((D_MODEL, D_MODEL), fix2)],
        out_specs=[pl.BlockSpec((SSM_ROWS * nlb, LANES), row),
                   pl.BlockSpec((SUBLANES, N_STATES), state_map),
                   pl.BlockSpec((SUBLANES, N_STATES), state_map)],
        out_shape=[jax.ShapeDtypeStruct((n * nlb, LANES), F32),
                   jax.ShapeDtypeStruct((n_state_rows, N_STATES), F32),
                   jax.ShapeDtypeStruct((n_state_rows, N_STATES), F32)],
        scratch_shapes=[pltpu.VMEM((SSM_ROWS, D_MODEL), F32),
                        pltpu.VMEM((1, SSM_ROWS, 2 * SSM_BLK_ST), F32),
                        pltpu.VMEM((SSM_ROWS, D_MODEL), F32),
                        pltpu.VMEM((SSM_NBLK, SSM_ROWS, SSM_BLK_ST), F32),
                        pltpu.VMEM((SSM_NBLK, SSM_ROWS, SSM_BLK_ST), F32),
                        pltpu.VMEM((SUBLANES, SSM_BLK_ST), F32),
                        pltpu.VMEM((SUBLANES, SSM_BLK_ST), F32),
                        pltpu.VMEM((SSM_NBLK, 1, SSM_BLK_ST), F32),
                        pltpu.VMEM((SSM_NBLK, 1, SSM_BLK_ST), F32)],
        compiler_params=_cparams("arbitrary"),
        name="s5_mixer",
    )(x.reshape(n * nlb, LANES), x0r, x0i, nmw, a_re, a_im, bm, cm, d, wglu, wgate)


def _rope_tables(pos):
    half = ROT_DIM // 2
    inv_freq = ROPE_THETA ** (-jnp.arange(half, dtype=F32) * 2.0 / ROT_DIM)
    ang = pos.astype(F32)[:, None] * inv_freq[None, :]
    cos, sin = jnp.cos(ang), jnp.sin(ang)
    n = pos.shape[0]
    pad = HEAD_DIM - ROT_DIM
    cosf = jnp.concatenate([cos, cos, jnp.ones((n, pad), F32)], axis=1)
    sina = jnp.concatenate([-sin, jnp.zeros((n, half + pad), F32)], axis=1)
    sinb = jnp.concatenate([jnp.zeros((n, half), F32), sin, jnp.zeros((n, pad), F32)], axis=1)
    tile2 = lambda t: jnp.concatenate([t, t], axis=1)
    return tile2(cosf), tile2(sina), tile2(sinb), cos.T, sin.T


def _s5_params(lam_re, lam_im, log_dt, b_re, b_im, c_re, c_im):
    dt = jnp.exp(log_dt)[:, None]
    z_re, z_im = lam_re * dt, lam_im * dt
    mag = jnp.exp(z_re)
    lb_re, lb_im = mag * jnp.cos(z_im), mag * jnp.sin(z_im)
    n_re, n_im = lb_re - 1.0, lb_im
    den = lam_re * lam_re + lam_im * lam_im
    k_re = (n_re * lam_re + n_im * lam_im) / den
    k_im = (n_im * lam_re - n_re * lam_im) / den
    bb_re = k_re[..., None] * b_re - k_im[..., None] * b_im
    bb_im = k_re[..., None] * b_im + k_im[..., None] * b_re
    gpb = SSM_BLK_CH // SSM_GROUP
    eye = jnp.eye(gpb, dtype=F32)

    def in_blocks(bb):
        v = bb.reshape(SSM_NBLK, gpb, SSM_STATE, SSM_GROUP).transpose(0, 1, 3, 2)
        return jnp.einsum('kgcp,gh->kgchp', v, eye).reshape(SSM_NBLK, SSM_BLK_CH, SSM_BLK_ST)

    def out_blocks(cc):
        v = cc.reshape(SSM_NBLK, gpb, SSM_GROUP, SSM_STATE).transpose(0, 1, 3, 2)
        return jnp.einsum('kgpc,gh->kgphc', v, eye).reshape(SSM_NBLK, SSM_BLK_ST, SSM_BLK_CH)

    bm = jnp.concatenate([in_blocks(bb_re), in_blocks(bb_im)], axis=2).astype(BF16)
    cm = jnp.concatenate([out_blocks(c_re), -out_blocks(c_im)], axis=1).astype(BF16)
    a_re = lb_re.reshape(SSM_NBLK, 1, SSM_BLK_ST)
    a_im = lb_im.reshape(SSM_NBLK, 1, SSM_BLK_ST)
    return a_re, a_im, bm, cm


def _router_params(w_rg, b_rg, w_re, b_re):
    pad = LANES - N_EXPERT_GROUPS - N_EXPERTS
    wr = jnp.concatenate([w_rg, w_re, jnp.zeros((D_MODEL, pad), F32)], axis=1)
    br = jnp.concatenate([b_rg, b_re, jnp.zeros((pad,), F32)])[None, :]
    return wr, br


def kernel(x_prompt, x_sample, cache_k, cache_v, state_ssm_re, state_ssm_im, norm_mix, norm_ffn, norm_final, attn_w_qkv, attn_lambda_q1, attn_lambda_k1, attn_lambda_q2, attn_lambda_k2, attn_subln, attn_w_o, ssm_lambda_re, ssm_lambda_im, ssm_log_dt, ssm_b_re, ssm_b_im, ssm_c_re, ssm_c_im, ssm_d, ssm_w_glu, ssm_w_gate, moe_w_router_group, moe_b_router_group, moe_w_router_expert, moe_b_router_expert, moe_w_gate, moe_w_up, moe_w_down):
    _, seq, _ = x_prompt.shape
    nb, dec, _ = x_sample.shape
    past = cache_k.shape[2]
    xp = x_prompt.reshape(seq, D_MODEL)
    xs = x_sample.reshape(nb * dec, D_MODEL)

    lam_init = 0.8 - 0.6 * math.exp(-0.3 * 0)
    lam = (jnp.exp(jnp.sum(attn_lambda_q1[0] * attn_lambda_k1[0]))
           - jnp.exp(jnp.sum(attn_lambda_q2[0] * attn_lambda_k2[0])) + lam_init)
    lam_p = jnp.full((1, ATT_T), lam, F32)
    lam_s = jnp.full((1, V_DIM), lam, F32)
    w_subln = attn_subln[0][None, :]
    out_scale = 1.0 - lam_init
    wqkv = attn_w_qkv[0].astype(BF16)
    wq, wk, wv = (wqkv[:, j * D_MODEL:(j + 1) * D_MODEL] for j in range(3))
    wo = attn_w_o[0].astype(BF16)
    nm0 = norm_mix[0][None, :]
    tabs_p = _rope_tables(jnp.arange(seq, dtype=jnp.int32))
    tabs_s = tuple(jnp.tile(t, (nb, 1))
                   for t in _rope_tables(past + jnp.arange(dec, dtype=jnp.int32))[:3])

    k_p, v_p, kb_p, qt_p, vt_p = _qkv_prompt(xp, nm0, wq.T, wk, wv, wv.T, *tabs_p, tm=512)
    q_s, k_s, v_s, kb_s, vb_s = _qkv(xs, nm0, wqkv, *tabs_s, tm=512)
    o_p = _attn_prompt(qt_p, kb_p, vt_p, w_subln, lam_p, out_scale)
    o_s = _attn_sample(q_s, kb_s, vb_s, cache_k[0].reshape(nb, past, D_MODEL),
                       cache_v[0].reshape(nb, past, D_MODEL), w_subln, lam_s, out_scale, t=dec)

    nf0 = norm_ffn[0][None, :]
    wr0, br0 = _router_params(moe_w_router_group[0], moe_b_router_group[0],
                              moe_w_router_expert[0], moe_b_router_expert[0])
    x1_p = _wo(o_p, xp, wo, tm=512)
    x1_s = _wo(o_s, xs, wo, tm=512)
    wg0, wu0, wd0 = (moe_w_gate[0].astype(BF16), moe_w_up[0].astype(BF16),
                     moe_w_down[0].astype(BF16))
    nfin = norm_final[None, :]
    xp1 = _moe(x1_p, nf0, wr0, br0, wg0, wu0, wd0, nfin, False, tm=1024)
    xs1 = _moe(x1_s, nf0, wr0, br0, wg0, wu0, wd0, nfin, False, tm=512)

    a_re, a_im, bm, cm = _s5_params(ssm_lambda_re[0], ssm_lambda_im[0], ssm_log_dt[0],
                                    ssm_b_re[0], ssm_b_im[0], ssm_c_re[0], ssm_c_im[0])
    nm1 = norm_mix[1][None, :]
    nf1 = norm_ffn[1][None, :]
    wr1, br1 = _router_params(moe_w_router_group[1], moe_b_router_group[1],
                              moe_w_router_expert[1], moe_b_router_expert[1])
    s5w = (nm1, a_re, a_im, bm, cm, ssm_d[0][None, :], ssm_w_glu[0].astype(BF16),
           ssm_w_gate[0].astype(BF16))
    zero_state = jnp.zeros((SUBLANES, N_STATES), F32)
    x2_p, sr_p, si_p = _s5(xp1, zero_state, zero_state, *s5w, chain=True)
    x0r = state_ssm_re[0].reshape(nb, N_STATES)
    x0i = state_ssm_im[0].reshape(nb, N_STATES)
    x2_s, sr_s, si_s = _s5(xs1, x0r, x0i, *s5w, chain=False)

    wg1, wu1, wd1 = (moe_w_gate[1].astype(BF16), moe_w_up[1].astype(BF16),
                     moe_w_down[1].astype(BF16))
    y_p = _moe(x2_p.reshape(seq, D_MODEL), nf1, wr1, br1, wg1, wu1, wd1, nfin, True, tm=1024)
    y_s = _moe(x2_s.reshape(nb * dec, D_MODEL), nf1, wr1, br1, wg1, wu1, wd1, nfin, True, tm=512)

    hshape = (N_HEADS, V_DIM)
    gshape = (N_SSM_GROUPS, SSM_STATE)
    return (y_p.reshape(1, seq, D_MODEL), y_s.reshape(nb, dec, D_MODEL),
            k_p.reshape(1, 1, seq, *hshape), v_p.reshape(1, 1, seq, *hshape),
            sr_p[SUBLANES - 1].reshape(1, 1, *gshape), si_p[SUBLANES - 1].reshape(1, 1, *gshape),
            k_s.reshape(1, nb, dec, *hshape), v_s.reshape(1, nb, dec, *hshape),
            sr_s.reshape(1, nb, *gshape), si_s.reshape(1, nb, *gshape))
```

```python
import functools
import math

import jax
import jax.numpy as jnp
import numpy as np
from jax import lax
from jax.experimental import pallas as pl
from jax.experimental.pallas import tpu as pltpu

F32 = jnp.float32
BF16 = jnp.bfloat16

D_MODEL = 1024
CHUNK = 64
N_HEADS = 8
HEAD_DIM = 64
V_DIM = 2 * HEAD_DIM
ROT_DIM = HEAD_DIM // 4
ROPE_THETA = 500000.0
SSM_GROUP = 16
N_SSM_GROUPS = D_MODEL // SSM_GROUP
SSM_STATE = 64
N_STATES = N_SSM_GROUPS * SSM_STATE
N_EXPERT_GROUPS = 4
EXPERTS_PER_GROUP = 4
N_EXPERTS = N_EXPERT_GROUPS * EXPERTS_PER_GROUP
D_EXPERT = D_MODEL // 4
RMS_EPS = 1e-6

LANES = 128
SUBLANES = 8
VMEM_LIMIT = 56 * 1024 * 1024

SSM_BLK_CH = 256
SSM_NBLK = D_MODEL // SSM_BLK_CH
SSM_BLK_ST = (SSM_BLK_CH // SSM_GROUP) * SSM_STATE
SSM_STEPS = 32
SSM_ROWS = SUBLANES * SSM_STEPS


def _cparams(*sem):
    return pltpu.CompilerParams(dimension_semantics=sem, vmem_limit_bytes=VMEM_LIMIT)


def _rms(x, w):
    return x * lax.rsqrt(jnp.mean(x * x, axis=-1, keepdims=True) + RMS_EPS) * w


def _qkv_kernel(x_ref, nw_ref, w_ref, cos_ref, sina_ref, sinb_ref,
                q_ref, k_ref, v_ref, kb_ref, vb_ref):
    h = _rms(x_ref[...], nw_ref[...]).astype(BF16)
    cosf, sina, sinb = cos_ref[...], sina_ref[...], sinb_ref[...]

    def rope(t):
        outs = []
        for hh in range(N_HEADS):
            xs = t[:, hh * LANES:(hh + 1) * LANES]
            outs.append(xs * cosf + pltpu.roll(xs, LANES - ROT_DIM // 2, 1) * sina
                        + pltpu.roll(xs, ROT_DIM // 2, 1) * sinb)
        return jnp.concatenate(outs, axis=1)

    q = rope(jnp.dot(h, w_ref[:, 0:D_MODEL], preferred_element_type=F32))
    q_ref[...] = (q * (HEAD_DIM ** -0.5)).astype(BF16)
    k = rope(jnp.dot(h, w_ref[:, D_MODEL:2 * D_MODEL], preferred_element_type=F32))
    k_ref[...] = k
    kb_ref[...] = k.astype(BF16)
    v = jnp.dot(h, w_ref[:, 2 * D_MODEL:3 * D_MODEL], preferred_element_type=F32)
    v_ref[...] = v
    vb_ref[...] = v.astype(BF16)


def _qkv(x, nw, w, cosf, sina, sinb, tm):
    n = x.shape[0]
    row = lambda i: (i, 0)
    fix = lambda i: (0, 0)
    return pl.pallas_call(
        _qkv_kernel,
        grid=(n // tm,),
        in_specs=[pl.BlockSpec((tm, D_MODEL), row), pl.BlockSpec((1, D_MODEL), fix),
                  pl.BlockSpec((D_MODEL, 3 * D_MODEL), fix),
                  pl.BlockSpec((tm, LANES), row), pl.BlockSpec((tm, LANES), row),
                  pl.BlockSpec((tm, LANES), row)],
        out_specs=[pl.BlockSpec((tm, D_MODEL), row)] * 5,
        out_shape=[jax.ShapeDtypeStruct((n, D_MODEL), BF16),
                   jax.ShapeDtypeStruct((n, D_MODEL), F32),
                   jax.ShapeDtypeStruct((n, D_MODEL), F32),
                   jax.ShapeDtypeStruct((n, D_MODEL), BF16),
                   jax.ShapeDtypeStruct((n, D_MODEL), BF16)],
        compiler_params=_cparams("arbitrary"),
        name="qkv_rope",
    )(x, nw, w, cosf, sina, sinb)


ATT_T = 512
ATT_K = 256
ATT_ONES = 16
ATT_UNROLL = 4


def _store_token_head_rows(ref, x):
    for r in range(x.shape[0] // SUBLANES):
        for hh in range(N_HEADS):
            ref[pl.ds(r * SUBLANES * N_HEADS + hh, SUBLANES, stride=N_HEADS), :] = (
                x[r * SUBLANES:(r + 1) * SUBLANES, hh * LANES:(hh + 1) * LANES])


def _qkv_prompt_kernel(x_ref, nw_ref, wqt_ref, wk_ref, wv_ref, wvt_ref,
                       cs_ref, expand_ref, cost_ref, sint_ref,
                       k_ref, v_ref, kb_ref, qt_ref, vt_ref):
    h = _rms(x_ref[...], nw_ref[...]).astype(BF16)
    tabs = jnp.dot(cs_ref[...], expand_ref[...], precision=lax.Precision.HIGHEST,
                   preferred_element_type=F32)
    lane = lax.broadcasted_iota(jnp.int32, (1, LANES), 1)
    cosf = tabs[:, 0:LANES] + jnp.where(lane % HEAD_DIM >= ROT_DIM, 1.0, 0.0)
    sina, sinb = tabs[:, LANES:2 * LANES], tabs[:, 2 * LANES:3 * LANES]
    outs = []
    k = jnp.dot(h, wk_ref[...], preferred_element_type=F32)
    for hh in range(N_HEADS):
        xs = k[:, hh * LANES:(hh + 1) * LANES]
        outs.append(xs * cosf + pltpu.roll(xs, LANES - ROT_DIM // 2, 1) * sina
                    + pltpu.roll(xs, ROT_DIM // 2, 1) * sinb)
    k = jnp.concatenate(outs, axis=1)
    _store_token_head_rows(k_ref, k)
    kb_ref[...] = k.astype(BF16)
    _store_token_head_rows(v_ref, jnp.dot(h, wv_ref[...], preferred_element_type=F32))

    nt = (((1,), (1,)), ((), ()))
    vt = lax.dot_general(wvt_ref[...], h, nt, preferred_element_type=F32).astype(BF16)
    qt = lax.dot_general(wqt_ref[...], h, nt, preferred_element_type=F32) * (
        HEAD_DIM ** -0.5 * math.log2(math.e))
    cost, sint = cost_ref[...], sint_ref[...]
    half = ROT_DIM // 2
    pieces = []
    for blk in range(D_MODEL // HEAD_DIM):
        r0 = blk * HEAD_DIM
        x1, x2 = qt[r0:r0 + half], qt[r0 + half:r0 + ROT_DIM]
        pieces += [x1 * cost - x2 * sint, x2 * cost + x1 * sint, qt[r0 + ROT_DIM:r0 + HEAD_DIM]]
    qt = jnp.concatenate(pieces, axis=0).astype(BF16)
    for b in range(qt_ref.shape[0]):
        qt_ref[b] = qt[:, b * ATT_T:(b + 1) * ATT_T]
    for b in range(vt_ref.shape[0]):
        vt_ref[b] = vt[:, b * ATT_K:(b + 1) * ATT_K]


def _qkv_prompt(x, nw, wqt, wk, wv, wvt, cs, expand, cost, sint, tm):
    n = x.shape[0]
    row = lambda i: (i, 0)
    fix = lambda i: (0, 0)
    wspec = pl.BlockSpec((D_MODEL, D_MODEL), fix)
    ttspec = pl.BlockSpec((ROT_DIM // 2, tm), lambda i: (0, i))
    thspec = pl.BlockSpec((tm * N_HEADS, V_DIM), row)
    qblk = pl.BlockSpec((tm // ATT_T, D_MODEL, ATT_T), lambda i: (i, 0, 0))
    vblk = pl.BlockSpec((tm // ATT_K, D_MODEL, ATT_K), lambda i: (i, 0, 0))
    return pl.pallas_call(
        _qkv_prompt_kernel,
        grid=(n // tm,),
        in_specs=[pl.BlockSpec((tm, D_MODEL), row), pl.BlockSpec((1, D_MODEL), fix),
                  wspec, wspec, wspec, wspec,
                  pl.BlockSpec((tm, ROT_DIM), row), pl.BlockSpec((ROT_DIM, 3 * LANES), fix),
                  ttspec, ttspec],
        out_specs=[thspec, thspec, pl.BlockSpec((tm, D_MODEL), row), qblk, vblk],
        out_shape=[jax.ShapeDtypeStruct((n * N_HEADS, V_DIM), F32),
                   jax.ShapeDtypeStruct((n * N_HEADS, V_DIM), F32),
                   jax.ShapeDtypeStruct((n, D_MODEL), BF16),
                   jax.ShapeDtypeStruct((n // ATT_T, D_MODEL, ATT_T), BF16),
                   jax.ShapeDtypeStruct((n // ATT_K, D_MODEL, ATT_K), BF16)],
        compiler_params=_cparams("arbitrary"),
        name="qkv_rope_prompt",
    )(x, nw, wqt, wk, wv, wvt, cs, expand, cost, sint)


def _stack_subheads(q):
    lane = lax.broadcasted_iota(jnp.int32, q.shape, 1)
    zero = jnp.zeros_like(q)
    return jnp.concatenate([jnp.where(lane < HEAD_DIM, q, zero),
                            jnp.where(lane >= HEAD_DIM, q, zero)], axis=0)


def _scores(qs, k):
    return lax.dot_general(qs, k, (((1,), (1,)), ((), ())), preferred_element_type=F32)


def _diff_finish(acc, l, t, lam, w, out_scale):
    o = acc[:t] / l[:t] - lam * (acc[t:] / l[t:])
    return _rms(o, w) * out_scale


def _attn_prompt_kernel(qt_ref, k_ref, vt_ref, w_ref, lam_ref, o_ref, sa_ref, sb_ref, acc_ref,
                        bias_ref, *, out_scale):
    t, tk = ATT_T, ATT_K
    i = pl.program_id(1)
    qt = qt_ref[0]
    row = lax.broadcasted_iota(jnp.int32, qt.shape, 0)
    zero = jnp.zeros_like(qt)
    qs = jnp.concatenate([jnp.where(row < HEAD_DIM, qt, zero),
                          jnp.where(row >= HEAD_DIM, qt, zero)], axis=1)

    def scores(j):
        r0 = pl.multiple_of(j * tk, tk)
        return jnp.dot(k_ref[pl.ds(r0, tk), :], qs, preferred_element_type=F32)

    ones = jnp.ones((ATT_ONES, tk), BF16)

    def update(m, s, j):
        m_new = jnp.maximum(m, jnp.max(s, axis=0, keepdims=True))
        alpha = jnp.exp2(m - m_new)
        p = jnp.exp2(s - m_new).astype(BF16)
        v1 = jnp.concatenate([vt_ref[j], ones], axis=0)
        acc_ref[...] = alpha * acc_ref[...] + jnp.dot(v1, p, preferred_element_type=F32)
        return m_new

    @pl.when((pl.program_id(0) == 0) & (i == 0))
    def _():
        for b in range(t // tk):
            kc = (b * tk + lax.broadcasted_iota(jnp.int32, (tk, 2 * t), 0)) // CHUNK
            qc = (lax.broadcasted_iota(jnp.int32, (tk, 2 * t), 1) % t) // CHUNK
            bias_ref[b] = jnp.where(kc <= qc, 0.0, -jnp.inf)

    acc_ref[...] = jnp.zeros(acc_ref.shape, F32)
    sa_ref[...] = scores(0)

    def pairs(n):
        def body(jj, m):
            for u in range(n):
                j = 2 * (jj * n + u)
                sb_ref[...] = scores(j + 1)
                m = update(m, sa_ref[...], j)
                sa_ref[...] = scores(j + 2)
                m = update(m, sb_ref[...], j + 1)
            return m
        return body

    m = jnp.full((1, 2 * t), -jnp.inf, F32)
    nlong = i // ATT_UNROLL
    m = lax.fori_loop(0, nlong, pairs(ATT_UNROLL), m)
    m = lax.fori_loop(nlong * ATT_UNROLL, i, pairs(1), m)

    sb_ref[...] = scores(2 * i + 1)
    m = update(m, sa_ref[...] + bias_ref[0], 2 * i)
    update(m, sb_ref[...] + bias_ref[1], 2 * i + 1)

    acc = acc_ref[0:V_DIM, :]
    l = acc_ref[V_DIM:V_DIM + 1, :]
    o = acc[:, :t] / l[:, :t] - lam_ref[...] * (acc[:, t:] / l[:, t:])
    o = o * lax.rsqrt(jnp.mean(o * o, axis=0, keepdims=True) + RMS_EPS)
    o_ref[...] = (o.T * w_ref[...] * out_scale).astype(BF16)


def _attn_prompt(qt, kb, vt, w_subln, lam, out_scale):
    l = kb.shape[0]
    t, tk = ATT_T, ATT_K
    kern = functools.partial(_attn_prompt_kernel, out_scale=out_scale)
    return pl.pallas_call(
        kern,
        grid=(N_HEADS, l // t),
        in_specs=[pl.BlockSpec((1, V_DIM, t), lambda h, i: (i, h, 0)),
                  pl.BlockSpec((l, V_DIM), lambda h, i: (0, h)),
                  pl.BlockSpec((l // tk, V_DIM, tk), lambda h, i: (0, h, 0)),
                  pl.BlockSpec((1, V_DIM), lambda h, i: (0, 0)),
                  pl.BlockSpec((1, t), lambda h, i: (0, 0))],
        out_specs=pl.BlockSpec((t, V_DIM), lambda h, i: (i, h)),
        out_shape=jax.ShapeDtypeStruct((l, D_MODEL), BF16),
        scratch_shapes=[pltpu.VMEM((tk, 2 * t), F32), pltpu.VMEM((tk, 2 * t), F32),
                        pltpu.VMEM((V_DIM + ATT_ONES, 2 * t), F32),
                        pltpu.VMEM((t // tk, tk, 2 * t), F32)],
        compiler_params=_cparams("arbitrary", "arbitrary"),
        name="attn_prompt",
    )(qt, kb, vt, w_subln, lam)


def _attn_sample_kernel(q_ref, kn_ref, vn_ref, ck_ref, cv_ref, w_ref, lam_ref, o_ref,
                        *, t, out_scale):
    for hh in range(N_HEADS):
        cols = slice(hh * V_DIM, (hh + 1) * V_DIM)
        qs = _stack_subheads(q_ref[:, cols])
        past = ck_ref.shape[1] // N_HEADS
        kc = ck_ref[0, pl.ds(hh, past, stride=N_HEADS), :].astype(BF16)
        vc = cv_ref[0, pl.ds(hh, past, stride=N_HEADS), :].astype(BF16)
        s_c = _scores(qs, kc)
        s_n = _scores(qs, kn_ref[:, cols])
        m = jnp.maximum(jnp.max(s_c, axis=-1, keepdims=True), jnp.max(s_n, axis=-1, keepdims=True))
        p_c = jnp.exp(s_c - m)
        p_n = jnp.exp(s_n - m)
        l = jnp.sum(p_c, axis=-1, keepdims=True) + jnp.sum(p_n, axis=-1, keepdims=True)
        acc = (jnp.dot(p_c.astype(BF16), vc, preferred_element_type=F32)
               + jnp.dot(p_n.astype(BF16), vn_ref[:, cols], preferred_element_type=F32))
        o_ref[:, cols] = _diff_finish(acc, l, t, lam_ref[...], w_ref[...], out_scale).astype(BF16)


def _attn_sample(q, kb, vb, cache_k, cache_v, w_subln, lam, out_scale, t):
    nb, rows, _ = cache_k.shape
    kern = functools.partial(_attn_sample_kernel, t=t, out_scale=out_scale)
    row = lambda b: (b, 0)
    return pl.pallas_call(
        kern,
        grid=(nb,),
        in_specs=[pl.BlockSpec((t, D_MODEL), row), pl.BlockSpec((t, D_MODEL), row),
                  pl.BlockSpec((t, D_MODEL), row),
                  pl.BlockSpec((1, rows, V_DIM), lambda b: (b, 0, 0)),
                  pl.BlockSpec((1, rows, V_DIM), lambda b: (b, 0, 0)),
                  pl.BlockSpec((1, V_DIM), lambda b: (0, 0)),
                  pl.BlockSpec((1, V_DIM), lambda b: (0, 0))],
        out_specs=pl.BlockSpec((t, D_MODEL), row),
        out_shape=jax.ShapeDtypeStruct((nb * t, D_MODEL), BF16),
        compiler_params=_cparams("arbitrary"),
        name="attn_sample",
    )(q, kb, vb, cache_k, cache_v, w_subln, lam)


def _route(x, nw, wr, br):
    hf = _rms(x, nw)
    h = hf.astype(BF16)
    h_lo = (hf - h.astype(F32)).astype(BF16)
    hw = jnp.dot(h, wr, preferred_element_type=F32)
    lg = (hw[:, :LANES] + hw[:, LANES:] + jnp.dot(h_lo, wr[:, :LANES], preferred_element_type=F32)
          + br)
    lane = lax.broadcasted_iota(jnp.int32, lg.shape, 1)
    big = jnp.int32(LANES)
    neg = jnp.float32(-jnp.inf)

    def masked_softmax(mask):
        z = jnp.where(mask, lg, neg)
        e = jnp.exp(z - jnp.max(z, axis=-1, keepdims=True))
        return e / jnp.sum(e, axis=-1, keepdims=True)

    def top1(p, mask):
        v = jnp.max(jnp.where(mask, p, -1.0), axis=-1, keepdims=True)
        idx = jnp.min(jnp.where(mask & (p == v), lane, big), axis=-1, keepdims=True)
        return v, idx

    gmask = lane < N_EXPERT_GROUPS
    g_val, g_idx = top1(masked_softmax(gmask), gmask)
    lo = N_EXPERT_GROUPS + g_idx * EXPERTS_PER_GROUP
    emask = (lane >= lo) & (lane < lo + EXPERTS_PER_GROUP)
    pe = masked_softmax(emask)
    v1, i1 = top1(pe, emask)
    v2, i2 = top1(pe, emask & (lane != i1))
    den = v1 + v2
    w1 = g_val * v1 / den
    w2 = g_val * v2 / den
    comb = (jnp.where(lane == i1 - N_EXPERT_GROUPS, w1, 0.0)
            + jnp.where(lane == i2 - N_EXPERT_GROUPS, w2, 0.0))
    return h, comb


def _wo_kernel(o_ref, x_ref, wo_ref, x1_ref):
    x1_ref[...] = x_ref[...] + jnp.dot(o_ref[...], wo_ref[...], preferred_element_type=F32)


def _wo(o, x, wo, tm):
    n = x.shape[0]
    row = lambda i: (i, 0)
    return pl.pallas_call(
        _wo_kernel,
        grid=(n // tm,),
        in_specs=[pl.BlockSpec((tm, D_MODEL), row), pl.BlockSpec((tm, D_MODEL), row),
                  pl.BlockSpec((D_MODEL, D_MODEL), lambda i: (0, 0))],
        out_specs=pl.BlockSpec((tm, D_MODEL), row),
        out_shape=jax.ShapeDtypeStruct((n, D_MODEL), F32),
        compiler_params=_cparams("arbitrary"),
        name="attn_out_proj",
    )(o, x, wo)


def _moe_kernel(x_ref, nfw_ref, wr_ref, br_ref, wg_ref, wu_ref, wd_ref, nw_ref, y_ref,
                h_ref, c_ref, *, final_norm):
    g = pl.program_id(1)

    @pl.when(g == 0)
    def _():
        x = x_ref[...]
        y_ref[...] = x
        h_ref[...], c_ref[...] = _route(x, nfw_ref[...], wr_ref[...], br_ref[...])

    h = h_ref[...]
    comb = c_ref[...]
    acc = y_ref[...]
    for e in range(EXPERTS_PER_GROUP):
        gate = jnp.dot(h, wg_ref[e], preferred_element_type=F32)
        up = jnp.dot(h, wu_ref[e], preferred_element_type=F32)
        hid = (gate * jax.nn.sigmoid(gate) * up).astype(BF16)
        lane = lax.broadcasted_iota(jnp.int32, comb.shape, 1)
        ce = jnp.sum(jnp.where(lane == g * EXPERTS_PER_GROUP + e, comb, 0.0),
                     axis=-1, keepdims=True)
        acc = acc + ce * jnp.dot(hid, wd_ref[e], preferred_element_type=F32)
    y_ref[...] = acc

    if final_norm:
        @pl.when(g == N_EXPERT_GROUPS - 1)
        def _():
            y_ref[...] = _rms(y_ref[...], nw_ref[...])


def _moe(x, nfw, wr, br, wg, wu, wd, nw, final_norm, tm):
    n = x.shape[0]
    row = lambda i, g: (i, 0)
    fix = lambda i, g: (0, 0)
    kern = functools.partial(_moe_kernel, final_norm=final_norm)
    return pl.pallas_call(
        kern,
        grid=(n // tm, N_EXPERT_GROUPS),
        in_specs=[pl.BlockSpec((tm, D_MODEL), row), pl.BlockSpec((1, D_MODEL), fix),
                  pl.BlockSpec((D_MODEL, 2 * LANES), fix), pl.BlockSpec((1, LANES), fix),
                  pl.BlockSpec((EXPERTS_PER_GROUP, D_MODEL, D_EXPERT), lambda i, g: (g, 0, 0)),
                  pl.BlockSpec((EXPERTS_PER_GROUP, D_MODEL, D_EXPERT), lambda i, g: (g, 0, 0)),
                  pl.BlockSpec((EXPERTS_PER_GROUP, D_EXPERT, D_MODEL), lambda i, g: (g, 0, 0)),
                  pl.BlockSpec((1, D_MODEL), lambda i, g: (0, 0))],
        out_specs=pl.BlockSpec((tm, D_MODEL), row),
        out_shape=jax.ShapeDtypeStruct((n, D_MODEL), F32),
        scratch_shapes=[pltpu.VMEM((tm, D_MODEL), BF16), pltpu.VMEM((tm, LANES), F32)],
        compiler_params=_cparams("arbitrary", "arbitrary"),
        name="moe",
    )(x, nfw, wr, br, wg, wu, wd, nw)


def _cmul(ar, ai, br, bi):
    return ar * br - ai * bi, ar * bi + ai * br


def _s5_kernel(x_ref, x0r_ref, x0i_ref, nmw_ref, ar_ref, ai_ref, bm_ref, cm_ref, d_ref,
               wglu_ref, wgate_ref,
               x2_ref, sr_ref, si_ref,
               xp_ref, bu_all_ref, y_ref, pr_ref, pi_ref, cr_ref, ci_ref, car_r_ref, car_i_ref,
               *, chain):
    steps, rows, nst = SSM_STEPS, SSM_ROWS, SSM_BLK_ST

    @pl.when(pl.program_id(0) == 0)
    def _():
        for kb in range(SSM_NBLK):
            ar, ai = ar_ref[kb], ai_ref[kb]
            pr, pi = ar, ai
            for i in range(steps):
                if i:
                    pr, pi = _cmul(ar, ai, pr, pi)
                pr_ref[kb, i * SUBLANES:(i + 1) * SUBLANES, :] = jnp.broadcast_to(pr, (SUBLANES, nst))
                pi_ref[kb, i * SUBLANES:(i + 1) * SUBLANES, :] = jnp.broadcast_to(pi, (SUBLANES, nst))
        car_r_ref[...] = jnp.zeros_like(car_r_ref)
        car_i_ref[...] = jnp.zeros_like(car_i_ref)

    nlb = D_MODEL // LANES
    for i in range(steps):
        for b in range(nlb):
            xp_ref[i * SUBLANES:(i + 1) * SUBLANES, b * LANES:(b + 1) * LANES] = (
                x_ref[pl.ds(i * nlb + b, SUBLANES, stride=steps * nlb), :])

    u = _rms(xp_ref[...], nmw_ref[...])
    ub = u.astype(BF16)

    half = nst // 2
    for kb in range(SSM_NBLK):
        bu_ref = bu_all_ref.at[kb % bu_all_ref.shape[0]]
        bu_ref[...] = jnp.dot(ub[:, kb * SSM_BLK_CH:(kb + 1) * SSM_BLK_CH], bm_ref[kb],
                              preferred_element_type=F32)
        for hh in range(2):
            re = slice(hh * half, (hh + 1) * half)
            im = slice(nst + hh * half, nst + (hh + 1) * half)
            ar = jnp.broadcast_to(ar_ref[kb, :, re], (SUBLANES, half))
            ai = jnp.broadcast_to(ai_ref[kb, :, re], (SUBLANES, half))

            def scan_step(i, s, re=re, im=im, ar=ar, ai=ai, bu_ref=bu_ref):
                sr, si = s
                r0 = pl.multiple_of(i * SUBLANES, SUBLANES)
                tr, ti = _cmul(ar, ai, sr, si)
                sr = tr + bu_ref[pl.ds(r0, SUBLANES), re]
                si = ti + bu_ref[pl.ds(r0, SUBLANES), im]
                bu_ref[pl.ds(r0, SUBLANES), re] = sr
                bu_ref[pl.ds(r0, SUBLANES), im] = si
                return sr, si

            zero = jnp.zeros((SUBLANES, half), F32)
            lax.fori_loop(0, steps, scan_step, (zero, zero), unroll=True)

        if chain:
            alr, ali = pr_ref[kb, rows - 1:rows, :], pi_ref[kb, rows - 1:rows, :]
            cr, ci = car_r_ref[kb], car_i_ref[kb]
            for j in range(SUBLANES):
                cr_ref[j:j + 1, :] = cr
                ci_ref[j:j + 1, :] = ci
                last = rows - SUBLANES + j
                tr, ti = _cmul(alr, ali, cr, ci)
                cr = tr + bu_ref[last:last + 1, 0:nst]
                ci = ti + bu_ref[last:last + 1, nst:2 * nst]
            car_r_ref[kb] = cr
            car_i_ref[kb] = ci
        else:
            cr_ref[...] = x0r_ref[:, kb * nst:(kb + 1) * nst]
            ci_ref[...] = x0i_ref[:, kb * nst:(kb + 1) * nst]

        c_r, c_i = cr_ref[...], ci_ref[...]

        def fix_step(i, _, kb=kb, c_r=c_r, c_i=c_i, bu_ref=bu_ref):
            r0 = pl.multiple_of(i * SUBLANES, SUBLANES)
            tr, ti = _cmul(pr_ref[kb, pl.ds(r0, SUBLANES), :], pi_ref[kb, pl.ds(r0, SUBLANES), :],
                           c_r, c_i)
            bu_ref[pl.ds(r0, SUBLANES), 0:nst] = bu_ref[pl.ds(r0, SUBLANES), 0:nst] + tr
            bu_ref[pl.ds(r0, SUBLANES), nst:2 * nst] = bu_ref[pl.ds(r0, SUBLANES), nst:2 * nst] + ti
            return 0

        lax.fori_loop(0, steps, fix_step, 0, unroll=True)
        sr_ref[:, kb * nst:(kb + 1) * nst] = bu_ref[rows - SUBLANES:rows, 0:nst]
        si_ref[:, kb * nst:(kb + 1) * nst] = bu_ref[rows - SUBLANES:rows, nst:2 * nst]
        y_ref[:, kb * SSM_BLK_CH:(kb + 1) * SSM_BLK_CH] = jnp.dot(
            bu_ref[...].astype(BF16), cm_ref[kb], preferred_element_type=F32)

    y = y_ref[...] + d_ref[...] * u
    z = jax.nn.gelu(y).astype(BF16)
    out = (jnp.dot(z, wglu_ref[...], preferred_element_type=F32)
           * jax.nn.sigmoid(jnp.dot(z, wgate_ref[...], preferred_element_type=F32)))
    y_ref[...] = xp_ref[...] + out
    for i in range(steps):
        for b in range(nlb):
            x2_ref[pl.ds(i * nlb + b, SUBLANES, stride=steps * nlb), :] = (
                y_ref[i * SUBLANES:(i + 1) * SUBLANES, b * LANES:(b + 1) * LANES])


def _s5(x, x0r, x0i, nmw, a_re, a_im, bm, cm, d, wglu, wgate, chain):
    n = x.shape[0]
    nlb = D_MODEL // LANES
    nchunk = n // SSM_ROWS
    row = lambda i: (i, 0)
    fix2 = lambda i: (0, 0)
    fix3 = lambda i: (0, 0, 0)
    state_map = fix2 if chain else row
    n_state_rows = SUBLANES if chain else SUBLANES * nchunk
    kern = functools.partial(_s5_kernel, chain=chain)
    return pl.pallas_call(
        kern,
        grid=(nchunk,),
        in_specs=[pl.BlockSpec((SSM_ROWS * nlb, LANES), row),
                  pl.BlockSpec((SUBLANES, N_STATES), state_map),
                  pl.BlockSpec((SUBLANES, N_STATES), state_map),
                  pl.BlockSpec((1, D_MODEL), fix2),
                  pl.BlockSpec((SSM_NBLK, 1, SSM_BLK_ST), fix3),
                  pl.BlockSpec((SSM_NBLK, 1, SSM_BLK_ST), fix3),
                  pl.BlockSpec((SSM_NBLK, SSM_BLK_CH, 2 * SSM_BLK_ST), fix3),
                  pl.BlockSpec((SSM_NBLK, 2 * SSM_BLK_ST, SSM_BLK_CH), fix3),
                  pl.BlockSpec((1, D_MODEL), fix2),
                  pl.BlockSpec((D_MODEL, D_MODEL), fix2),
                  pl.BlockSpec((D_MODEL, D_MODEL), fix2)],
        out_specs=[pl.BlockSpec((SSM_ROWS * nlb, LANES), row),
                   pl.BlockSpec((SUBLANES, N_STATES), state_map),
                   pl.BlockSpec((SUBLANES, N_STATES), state_map)],
        out_shape=[jax.ShapeDtypeStruct((n * nlb, LANES), F32),
                   jax.ShapeDtypeStruct((n_state_rows, N_STATES), F32),
                   jax.ShapeDtypeStruct((n_state_rows, N_STATES), F32)],
        scratch_shapes=[pltpu.VMEM((SSM_ROWS, D_MODEL), F32),
                        pltpu.VMEM((1, SSM_ROWS, 2 * SSM_BLK_ST), F32),
                        pltpu.VMEM((SSM_ROWS, D_MODEL), F32),
                        pltpu.VMEM((SSM_NBLK, SSM_ROWS, SSM_BLK_ST), F32),
                        pltpu.VMEM((SSM_NBLK, SSM_ROWS, SSM_BLK_ST), F32),
                        pltpu.VMEM((SUBLANES, SSM_BLK_ST), F32),
                        pltpu.VMEM((SUBLANES, SSM_BLK_ST), F32),
                        pltpu.VMEM((SSM_NBLK, 1, SSM_BLK_ST), F32),
                        pltpu.VMEM((SSM_NBLK, 1, SSM_BLK_ST), F32)],
        compiler_params=_cparams("arbitrary"),
        name="s5_mixer",
    )(x.reshape(n * nlb, LANES), x0r, x0i, nmw, a_re, a_im, bm, cm, d, wglu, wgate)


def _rope_cos_sin(pos):
    half = ROT_DIM // 2
    inv_freq = ROPE_THETA ** (-jnp.arange(half, dtype=F32) * 2.0 / ROT_DIM)
    ang = pos.astype(F32)[:, None] * inv_freq[None, :]
    return jnp.cos(ang), jnp.sin(ang)


def _rope_expand_matrix():
    half = ROT_DIM // 2
    e = np.zeros((ROT_DIM, 3 * LANES), np.float32)
    for lane in range(LANES):
        d = lane % HEAD_DIM
        if d < ROT_DIM:
            e[d % half, lane] = 1.0
        if d < half:
            e[half + d, LANES + lane] = -1.0
        elif d < ROT_DIM:
            e[half + d - half, 2 * LANES + lane] = 1.0
    return jnp.asarray(e)


def _rope_tables(pos):
    half = ROT_DIM // 2
    cos, sin = _rope_cos_sin(pos)
    n = pos.shape[0]
    pad = HEAD_DIM - ROT_DIM
    cosf = jnp.concatenate([cos, cos, jnp.ones((n, pad), F32)], axis=1)
    sina = jnp.concatenate([-sin, jnp.zeros((n, half + pad), F32)], axis=1)
    sinb = jnp.concatenate([jnp.zeros((n, half), F32), sin, jnp.zeros((n, pad), F32)], axis=1)
    tile2 = lambda t: jnp.concatenate([t, t], axis=1)
    return tile2(cosf), tile2(sina), tile2(sinb)


def _s5_params(lam_re, lam_im, log_dt, b_re, b_im, c_re, c_im):
    dt = jnp.exp(log_dt)[:, None]
    z_re, z_im = lam_re * dt, lam_im * dt
    mag = jnp.exp(z_re)
    lb_re, lb_im = mag * jnp.cos(z_im), mag * jnp.sin(z_im)
    n_re, n_im = lb_re - 1.0, lb_im
    den = lam_re * lam_re + lam_im * lam_im
    k_re = (n_re * lam_re + n_im * lam_im) / den
    k_im = (n_im * lam_re - n_re * lam_im) / den
    bb_re = k_re[..., None] * b_re - k_im[..., None] * b_im
    bb_im = k_re[..., None] * b_im + k_im[..., None] * b_re
    gpb = SSM_BLK_CH // SSM_GROUP
    eye = jnp.eye(gpb, dtype=F32)

    def in_blocks(bb):
        v = bb.reshape(SSM_NBLK, gpb, SSM_STATE, SSM_GROUP).transpose(0, 1, 3, 2)
        return jnp.einsum('kgcp,gh->kgchp', v, eye).reshape(SSM_NBLK, SSM_BLK_CH, SSM_BLK_ST)

    def out_blocks(cc):
        v = cc.reshape(SSM_NBLK, gpb, SSM_GROUP, SSM_STATE).transpose(0, 1, 3, 2)
        return jnp.einsum('kgpc,gh->kgphc', v, eye).reshape(SSM_NBLK, SSM_BLK_ST, SSM_BLK_CH)

    bm = jnp.concatenate([in_blocks(bb_re), in_blocks(bb_im)], axis=2).astype(BF16)
    cm = jnp.concatenate([out_blocks(c_re), -out_blocks(c_im)], axis=1).astype(BF16)
    a_re = lb_re.reshape(SSM_NBLK, 1, SSM_BLK_ST)
    a_im = lb_im.reshape(SSM_NBLK, 1, SSM_BLK_ST)
    return a_re, a_im, bm, cm


def _router_params(w_rg, b_rg, w_re, b_re):
    pad = LANES - N_EXPERT_GROUPS - N_EXPERTS
    wr = jnp.concatenate([w_rg, w_re, jnp.zeros((D_MODEL, pad), F32)], axis=1)
    w_hi = wr.astype(BF16)
    wr = jnp.concatenate([w_hi, (wr - w_hi.astype(F32)).astype(BF16)], axis=1)
    br = jnp.concatenate([b_rg, b_re, jnp.zeros((pad,), F32)])[None, :]
    return wr, br


def kernel(x_prompt, x_sample, cache_k, cache_v, state_ssm_re, state_ssm_im, norm_mix, norm_ffn, norm_final, attn_w_qkv, attn_lambda_q1, attn_lambda_k1, attn_lambda_q2, attn_lambda_k2, attn_subln, attn_w_o, ssm_lambda_re, ssm_lambda_im, ssm_log_dt, ssm_b_re, ssm_b_im, ssm_c_re, ssm_c_im, ssm_d, ssm_w_glu, ssm_w_gate, moe_w_router_group, moe_b_router_group, moe_w_router_expert, moe_b_router_expert, moe_w_gate, moe_w_up, moe_w_down):
    _, seq, _ = x_prompt.shape
    nb, dec, _ = x_sample.shape
    past = cache_k.shape[2]
    xp = x_prompt.reshape(seq, D_MODEL)
    xs = x_sample.reshape(nb * dec, D_MODEL)

    lam_init = 0.8 - 0.6 * math.exp(-0.3 * 0)
    lam = (jnp.exp(jnp.sum(attn_lambda_q1[0] * attn_lambda_k1[0]))
           - jnp.exp(jnp.sum(attn_lambda_q2[0] * attn_lambda_k2[0])) + lam_init)
    lam_p = jnp.full((1, ATT_T), lam, F32)
    lam_s = jnp.full((1, V_DIM), lam, F32)
    w_subln = attn_subln[0][None, :]
    out_scale = 1.0 - lam_init
    wqkv = attn_w_qkv[0].astype(BF16)
    wq, wk, wv = (wqkv[:, j * D_MODEL:(j + 1) * D_MODEL] for j in range(3))
    wo = attn_w_o[0].astype(BF16)
    nm0 = norm_mix[0][None, :]
    cos_p, sin_p = _rope_cos_sin(jnp.arange(seq, dtype=jnp.int32))
    tabs_s = tuple(jnp.tile(t, (nb, 1))
                   for t in _rope_tables(past + jnp.arange(dec, dtype=jnp.int32)))

    k_p, v_p, kb_p, qt_p, vt_p = _qkv_prompt(
        xp, nm0, wq.T, wk, wv, wv.T, jnp.concatenate([cos_p, sin_p], axis=1),
        _rope_expand_matrix(), cos_p.T, sin_p.T, tm=512)
    q_s, k_s, v_s, kb_s, vb_s = _qkv(xs, nm0, wqkv, *tabs_s, tm=512)
    o_p = _attn_prompt(qt_p, kb_p, vt_p, w_subln, lam_p, out_scale)
    o_s = _attn_sample(q_s, kb_s, vb_s, cache_k[0].reshape(nb, past * N_HEADS, V_DIM),
                       cache_v[0].reshape(nb, past * N_HEADS, V_DIM), w_subln, lam_s, out_scale,
                       t=dec)

    nf0 = norm_ffn[0][None, :]
    wr0, br0 = _router_params(moe_w_router_group[0], moe_b_router_group[0],
                              moe_w_router_expert[0], moe_b_router_expert[0])
    x1_p = _wo(o_p, xp, wo, tm=512)
    x1_s = _wo(o_s, xs, wo, tm=512)
    wg0, wu0, wd0 = (moe_w_gate[0].astype(BF16), moe_w_up[0].astype(BF16),
                     moe_w_down[0].astype(BF16))
    nfin = norm_final[None, :]
    xp1 = _moe(x1_p, nf0, wr0, br0, wg0, wu0, wd0, nfin, False, tm=1024)
    xs1 = _moe(x1_s, nf0, wr0, br0, wg0, wu0, wd0, nfin, False, tm=512)

    a_re, a_im, bm, cm = _s5_params(ssm_lambda_re[0], ssm_lambda_im[0], ssm_log_dt[0],
                                    ssm_b_re[0], ssm_b_im[0], ssm_c_re[0], ssm_c_im[0])
    nm1 = norm_mix[1][None, :]
    nf1 = norm_ffn[1][None, :]
    wr1, br1 = _router_params(moe_w_router_group[1], moe_b_router_group[1],
                              moe_w_router_expert[1], moe_b_router_expert[1])
    s5w = (nm1, a_re, a_im, bm, cm, ssm_d[0][None, :], ssm_w_glu[0].astype(BF16),
           ssm_w_gate[0].astype(BF16))
    zero_state = jnp.zeros((SUBLANES, N_STATES), F32)
    x2_p, sr_p, si_p = _s5(xp1, zero_state, zero_state, *s5w, chain=True)
    x0r = state_ssm_re[0].reshape(nb, N_STATES)
    x0i = state_ssm_im[0].reshape(nb, N_STATES)
    x2_s, sr_s, si_s = _s5(xs1, x0r, x0i, *s5w, chain=False)

    wg1, wu1, wd1 = (moe_w_gate[1].astype(BF16), moe_w_up[1].astype(BF16),
                     moe_w_down[1].astype(BF16))
    y_p = _moe(x2_p.reshape(seq, D_MODEL), nf1, wr1, br1, wg1, wu1, wd1, nfin, True, tm=1024)
    y_s = _moe(x2_s.reshape(nb * dec, D_MODEL), nf1, wr1, br1, wg1, wu1, wd1, nfin, True, tm=512)

    hshape = (N_HEADS, V_DIM)
    gshape = (N_SSM_GROUPS, SSM_STATE)
    return (y_p.reshape(1, seq, D_MODEL), y_s.reshape(nb, dec, D_MODEL),
            k_p.reshape(1, 1, seq, *hshape), v_p.reshape(1, 1, seq, *hshape),
            sr_p[SUBLANES - 1].reshape(1, 1, *gshape), si_p[SUBLANES - 1].reshape(1, 1, *gshape),
            k_s.reshape(1, nb, dec, *hshape), v_s.reshape(1, nb, dec, *hshape),
            sr_s.reshape(1, nb, *gshape), si_s.reshape(1, nb, *gshape))
```

```python
import functools
import math

import jax
import jax.numpy as jnp
import numpy as np
from jax import lax
from jax.experimental import pallas as pl
from jax.experimental.pallas import tpu as pltpu

F32 = jnp.float32
BF16 = jnp.bfloat16

D_MODEL = 1024
CHUNK = 64
N_HEADS = 8
HEAD_DIM = 64
V_DIM = 2 * HEAD_DIM
ROT_DIM = HEAD_DIM // 4
ROPE_THETA = 500000.0
SSM_GROUP = 16
N_SSM_GROUPS = D_MODEL // SSM_GROUP
SSM_STATE = 64
N_STATES = N_SSM_GROUPS * SSM_STATE
N_EXPERT_GROUPS = 4
EXPERTS_PER_GROUP = 4
N_EXPERTS = N_EXPERT_GROUPS * EXPERTS_PER_GROUP
D_EXPERT = D_MODEL // 4
RMS_EPS = 1e-6

LANES = 128
SUBLANES = 8
VMEM_LIMIT = 56 * 1024 * 1024

SSM_BLK_CH = 256
SSM_NBLK = D_MODEL // SSM_BLK_CH
SSM_BLK_ST = (SSM_BLK_CH // SSM_GROUP) * SSM_STATE
SSM_STEPS = 32
SSM_ROWS = SUBLANES * SSM_STEPS


def _cparams(*sem, flags=None):
    return pltpu.CompilerParams(dimension_semantics=sem, vmem_limit_bytes=VMEM_LIMIT, flags=flags)


def _rms(x, w):
    return x * lax.rsqrt(jnp.mean(x * x, axis=-1, keepdims=True) + RMS_EPS) * w


def _qkv_kernel(x_ref, nw_ref, w_ref, cos_ref, sina_ref, sinb_ref,
                q_ref, k_ref, v_ref, kb_ref, vb_ref):
    h = _rms(x_ref[...], nw_ref[...]).astype(BF16)
    cosf, sina, sinb = cos_ref[...], sina_ref[...], sinb_ref[...]

    def rope(t):
        outs = []
        for hh in range(N_HEADS):
            xs = t[:, hh * LANES:(hh + 1) * LANES]
            outs.append(xs * cosf + pltpu.roll(xs, LANES - ROT_DIM // 2, 1) * sina
                        + pltpu.roll(xs, ROT_DIM // 2, 1) * sinb)
        return jnp.concatenate(outs, axis=1)

    q = rope(jnp.dot(h, w_ref[:, 0:D_MODEL], preferred_element_type=F32))
    q_ref[...] = (q * (HEAD_DIM ** -0.5)).astype(BF16)
    k = rope(jnp.dot(h, w_ref[:, D_MODEL:2 * D_MODEL], preferred_element_type=F32))
    k_ref[...] = k
    kb_ref[...] = k.astype(BF16)
    v = jnp.dot(h, w_ref[:, 2 * D_MODEL:3 * D_MODEL], preferred_element_type=F32)
    v_ref[...] = v
    vb_ref[...] = v.astype(BF16)


def _qkv(x, nw, w, cosf, sina, sinb, tm):
    n = x.shape[0]
    row = lambda i: (i, 0)
    fix = lambda i: (0, 0)
    return pl.pallas_call(
        _qkv_kernel,
        grid=(n // tm,),
        in_specs=[pl.BlockSpec((tm, D_MODEL), row), pl.BlockSpec((1, D_MODEL), fix),
                  pl.BlockSpec((D_MODEL, 3 * D_MODEL), fix),
                  pl.BlockSpec((tm, LANES), row), pl.BlockSpec((tm, LANES), row),
                  pl.BlockSpec((tm, LANES), row)],
        out_specs=[pl.BlockSpec((tm, D_MODEL), row)] * 5,
        out_shape=[jax.ShapeDtypeStruct((n, D_MODEL), BF16),
                   jax.ShapeDtypeStruct((n, D_MODEL), F32),
                   jax.ShapeDtypeStruct((n, D_MODEL), F32),
                   jax.ShapeDtypeStruct((n, D_MODEL), BF16),
                   jax.ShapeDtypeStruct((n, D_MODEL), BF16)],
        compiler_params=_cparams("arbitrary"),
        name="qkv_rope",
    )(x, nw, w, cosf, sina, sinb)


ATT_T = 512
ATT_K = 256
ATT_ONES = 16
ATT_UNROLL = 4


def _store_token_head_rows(ref, x):
    for r in range(x.shape[0] // SUBLANES):
        for hh in range(N_HEADS):
            ref[pl.ds(r * SUBLANES * N_HEADS + hh, SUBLANES, stride=N_HEADS), :] = (
                x[r * SUBLANES:(r + 1) * SUBLANES, hh * LANES:(hh + 1) * LANES])


def _qkv_prompt_kernel(x_ref, nw_ref, wqt_ref, wk_ref, wv_ref, wvt_ref,
                       cs_ref, expand_ref, cost_ref, sint_ref,
                       k_ref, v_ref, kb_ref, qt_ref, vt_ref):
    h = _rms(x_ref[...], nw_ref[...]).astype(BF16)
    tabs = jnp.dot(cs_ref[...], expand_ref[...], precision=lax.Precision.HIGHEST,
                   preferred_element_type=F32)
    lane = lax.broadcasted_iota(jnp.int32, (1, LANES), 1)
    cosf = tabs[:, 0:LANES] + jnp.where(lane % HEAD_DIM >= ROT_DIM, 1.0, 0.0)
    sina, sinb = tabs[:, LANES:2 * LANES], tabs[:, 2 * LANES:3 * LANES]
    outs = []
    k = jnp.dot(h, wk_ref[...], preferred_element_type=F32)
    for hh in range(N_HEADS):
        xs = k[:, hh * LANES:(hh + 1) * LANES]
        outs.append(xs * cosf + pltpu.roll(xs, LANES - ROT_DIM // 2, 1) * sina
                    + pltpu.roll(xs, ROT_DIM // 2, 1) * sinb)
    k = jnp.concatenate(outs, axis=1)
    _store_token_head_rows(k_ref, k)
    kb_ref[...] = k.astype(BF16)
    _store_token_head_rows(v_ref, jnp.dot(h, wv_ref[...], preferred_element_type=F32))

    nt = (((1,), (1,)), ((), ()))
    vt = lax.dot_general(wvt_ref[...], h, nt, preferred_element_type=F32).astype(BF16)
    qt = lax.dot_general(wqt_ref[...], h, nt, preferred_element_type=F32) * (
        HEAD_DIM ** -0.5 * math.log2(math.e))
    cost, sint = cost_ref[...], sint_ref[...]
    half = ROT_DIM // 2
    pieces = []
    for blk in range(D_MODEL // HEAD_DIM):
        r0 = blk * HEAD_DIM
        x1, x2 = qt[r0:r0 + half], qt[r0 + half:r0 + ROT_DIM]
        pieces += [x1 * cost - x2 * sint, x2 * cost + x1 * sint, qt[r0 + ROT_DIM:r0 + HEAD_DIM]]
    qt = jnp.concatenate(pieces, axis=0).astype(BF16)
    for b in range(qt_ref.shape[0]):
        qt_ref[b] = qt[:, b * ATT_T:(b + 1) * ATT_T]
    for b in range(vt_ref.shape[0]):
        vt_ref[b] = vt[:, b * ATT_K:(b + 1) * ATT_K]


def _qkv_prompt(x, nw, wqt, wk, wv, wvt, cs, expand, cost, sint, tm):
    n = x.shape[0]
    row = lambda i: (i, 0)
    fix = lambda i: (0, 0)
    wspec = pl.BlockSpec((D_MODEL, D_MODEL), fix)
    ttspec = pl.BlockSpec((ROT_DIM // 2, tm), lambda i: (0, i))
    thspec = pl.BlockSpec((tm * N_HEADS, V_DIM), row)
    qblk = pl.BlockSpec((tm // ATT_T, D_MODEL, ATT_T), lambda i: (i, 0, 0))
    vblk = pl.BlockSpec((tm // ATT_K, D_MODEL, ATT_K), lambda i: (i, 0, 0))
    return pl.pallas_call(
        _qkv_prompt_kernel,
        grid=(n // tm,),
        in_specs=[pl.BlockSpec((tm, D_MODEL), row), pl.BlockSpec((1, D_MODEL), fix),
                  wspec, wspec, wspec, wspec,
                  pl.BlockSpec((tm, ROT_DIM), row), pl.BlockSpec((ROT_DIM, 3 * LANES), fix),
                  ttspec, ttspec],
        out_specs=[thspec, thspec, pl.BlockSpec((tm, D_MODEL), row), qblk, vblk],
        out_shape=[jax.ShapeDtypeStruct((n * N_HEADS, V_DIM), F32),
                   jax.ShapeDtypeStruct((n * N_HEADS, V_DIM), F32),
                   jax.ShapeDtypeStruct((n, D_MODEL), BF16),
                   jax.ShapeDtypeStruct((n // ATT_T, D_MODEL, ATT_T), BF16),
                   jax.ShapeDtypeStruct((n // ATT_K, D_MODEL, ATT_K), BF16)],
        compiler_params=_cparams("arbitrary"),
        name="qkv_rope_prompt",
    )(x, nw, wqt, wk, wv, wvt, cs, expand, cost, sint)


def _stack_subheads(q):
    lane = lax.broadcasted_iota(jnp.int32, q.shape, 1)
    zero = jnp.zeros_like(q)
    return jnp.concatenate([jnp.where(lane < HEAD_DIM, q, zero),
                            jnp.where(lane >= HEAD_DIM, q, zero)], axis=0)


def _scores(qs, k):
    return lax.dot_general(qs, k, (((1,), (1,)), ((), ())), preferred_element_type=F32)


def _diff_finish(acc, l, t, lam, w, out_scale):
    o = acc[:t] / l[:t] - lam * (acc[t:] / l[t:])
    return _rms(o, w) * out_scale


def _attn_prompt_kernel(qt_ref, k_ref, vt_ref, w_ref, lam_ref, o_ref, sa_ref, sb_ref, acc_ref,
                        bias_ref, *, out_scale):
    t, tk = ATT_T, ATT_K
    i = pl.program_id(1)
    qt = qt_ref[0]
    row = lax.broadcasted_iota(jnp.int32, qt.shape, 0)
    zero = jnp.zeros_like(qt)
    qs = jnp.concatenate([jnp.where(row < HEAD_DIM, qt, zero),
                          jnp.where(row >= HEAD_DIM, qt, zero)], axis=1)

    def scores(j):
        r0 = pl.multiple_of(j * tk, tk)
        return jnp.dot(k_ref[pl.ds(r0, tk), :], qs, preferred_element_type=F32)

    ones = jnp.ones((ATT_ONES, tk), BF16)

    def update(m, s, j):
        m_new = jnp.maximum(m, jnp.max(s, axis=0, keepdims=True))
        alpha = jnp.exp2(m - m_new)
        p = jnp.exp2(s - m_new).astype(BF16)
        v1 = jnp.concatenate([vt_ref[j], ones], axis=0)
        acc_ref[...] = alpha * acc_ref[...] + jnp.dot(v1, p, preferred_element_type=F32)
        return m_new

    @pl.when((pl.program_id(0) == 0) & (i == 0))
    def _():
        for b in range(t // tk):
            kc = (b * tk + lax.broadcasted_iota(jnp.int32, (tk, 2 * t), 0)) // CHUNK
            qc = (lax.broadcasted_iota(jnp.int32, (tk, 2 * t), 1) % t) // CHUNK
            bias_ref[b] = jnp.where(kc <= qc, 0.0, -jnp.inf)

    acc_ref[...] = jnp.zeros(acc_ref.shape, F32)
    sa_ref[...] = scores(0)

    def pairs(n):
        def body(jj, m):
            for u in range(n):
                j = 2 * (jj * n + u)
                sb_ref[...] = scores(j + 1)
                m = update(m, sa_ref[...], j)
                sa_ref[...] = scores(j + 2)
                m = update(m, sb_ref[...], j + 1)
            return m
        return body

    m = jnp.full((1, 2 * t), -jnp.inf, F32)
    nlong = i // ATT_UNROLL
    m = lax.fori_loop(0, nlong, pairs(ATT_UNROLL), m)
    m = lax.fori_loop(nlong * ATT_UNROLL, i, pairs(1), m)

    sb_ref[...] = scores(2 * i + 1)
    m = update(m, sa_ref[...] + bias_ref[0], 2 * i)
    update(m, sb_ref[...] + bias_ref[1], 2 * i + 1)

    acc = acc_ref[0:V_DIM, :]
    l = acc_ref[V_DIM:V_DIM + 1, :]
    o = acc[:, :t] / l[:, :t] - lam_ref[...] * (acc[:, t:] / l[:, t:])
    o = o * lax.rsqrt(jnp.mean(o * o, axis=0, keepdims=True) + RMS_EPS)
    o_ref[...] = (o.T * w_ref[...] * out_scale).astype(BF16)


def _attn_prompt(qt, kb, vt, w_subln, lam, out_scale):
    l = kb.shape[0]
    t, tk = ATT_T, ATT_K
    kern = functools.partial(_attn_prompt_kernel, out_scale=out_scale)
    return pl.pallas_call(
        kern,
        grid=(N_HEADS, l // t),
        in_specs=[pl.BlockSpec((1, V_DIM, t), lambda h, i: (i, h, 0)),
                  pl.BlockSpec((l, V_DIM), lambda h, i: (0, h)),
                  pl.BlockSpec((l // tk, V_DIM, tk), lambda h, i: (0, h, 0)),
                  pl.BlockSpec((1, V_DIM), lambda h, i: (0, 0)),
                  pl.BlockSpec((1, t), lambda h, i: (0, 0))],
        out_specs=pl.BlockSpec((t, V_DIM), lambda h, i: (i, h)),
        out_shape=jax.ShapeDtypeStruct((l, D_MODEL), BF16),
        scratch_shapes=[pltpu.VMEM((tk, 2 * t), F32), pltpu.VMEM((tk, 2 * t), F32),
                        pltpu.VMEM((V_DIM + ATT_ONES, 2 * t), F32),
                        pltpu.VMEM((t // tk, tk, 2 * t), F32)],
        compiler_params=_cparams("arbitrary", "arbitrary"),
        name="attn_prompt",
    )(qt, kb, vt, w_subln, lam)


def _attn_sample_kernel(q_ref, kn_ref, vn_ref, ck_ref, cv_ref, w_ref, lam_ref, o_ref,
                        *, t, out_scale):
    for hh in range(N_HEADS):
        cols = slice(hh * V_DIM, (hh + 1) * V_DIM)
        qs = _stack_subheads(q_ref[:, cols])
        past = ck_ref.shape[1] // N_HEADS
        kc = ck_ref[0, pl.ds(hh, past, stride=N_HEADS), :].astype(BF16)
        vc = cv_ref[0, pl.ds(hh, past, stride=N_HEADS), :].astype(BF16)
        s_c = _scores(qs, kc)
        s_n = _scores(qs, kn_ref[:, cols])
        m = jnp.maximum(jnp.max(s_c, axis=-1, keepdims=True), jnp.max(s_n, axis=-1, keepdims=True))
        p_c = jnp.exp(s_c - m)
        p_n = jnp.exp(s_n - m)
        l = jnp.sum(p_c, axis=-1, keepdims=True) + jnp.sum(p_n, axis=-1, keepdims=True)
        acc = (jnp.dot(p_c.astype(BF16), vc, preferred_element_type=F32)
               + jnp.dot(p_n.astype(BF16), vn_ref[:, cols], preferred_element_type=F32))
        o_ref[:, cols] = _diff_finish(acc, l, t, lam_ref[...], w_ref[...], out_scale).astype(BF16)


def _attn_sample(q, kb, vb, cache_k, cache_v, w_subln, lam, out_scale, t):
    nb, rows, _ = cache_k.shape
    kern = functools.partial(_attn_sample_kernel, t=t, out_scale=out_scale)
    row = lambda b: (b, 0)
    return pl.pallas_call(
        kern,
        grid=(nb,),
        in_specs=[pl.BlockSpec((t, D_MODEL), row), pl.BlockSpec((t, D_MODEL), row),
                  pl.BlockSpec((t, D_MODEL), row),
                  pl.BlockSpec((1, rows, V_DIM), lambda b: (b, 0, 0)),
                  pl.BlockSpec((1, rows, V_DIM), lambda b: (b, 0, 0)),
                  pl.BlockSpec((1, V_DIM), lambda b: (0, 0)),
                  pl.BlockSpec((1, V_DIM), lambda b: (0, 0))],
        out_specs=pl.BlockSpec((t, D_MODEL), row),
        out_shape=jax.ShapeDtypeStruct((nb * t, D_MODEL), BF16),
        compiler_params=_cparams("arbitrary"),
        name="attn_sample",
    )(q, kb, vb, cache_k, cache_v, w_subln, lam)


def _route(x, nw, wr, br):
    hf = _rms(x, nw)
    h = hf.astype(BF16)
    h_lo = (hf - h.astype(F32)).astype(BF16)
    hw = jnp.dot(h, wr, preferred_element_type=F32)
    lg = (hw[:, :LANES] + hw[:, LANES:] + jnp.dot(h_lo, wr[:, :LANES], preferred_element_type=F32)
          + br)
    lane = lax.broadcasted_iota(jnp.int32, lg.shape, 1)
    big = jnp.int32(LANES)
    neg = jnp.float32(-jnp.inf)

    def masked_softmax(mask):
        z = jnp.where(mask, lg, neg)
        e = jnp.exp(z - jnp.max(z, axis=-1, keepdims=True))
        return e / jnp.sum(e, axis=-1, keepdims=True)

    def top1(p, mask):
        v = jnp.max(jnp.where(mask, p, -1.0), axis=-1, keepdims=True)
        idx = jnp.min(jnp.where(mask & (p == v), lane, big), axis=-1, keepdims=True)
        return v, idx

    gmask = lane < N_EXPERT_GROUPS
    g_val, g_idx = top1(masked_softmax(gmask), gmask)
    lo = N_EXPERT_GROUPS + g_idx * EXPERTS_PER_GROUP
    emask = (lane >= lo) & (lane < lo + EXPERTS_PER_GROUP)
    pe = masked_softmax(emask)
    v1, i1 = top1(pe, emask)
    v2, i2 = top1(pe, emask & (lane != i1))
    den = v1 + v2
    w1 = g_val * v1 / den
    w2 = g_val * v2 / den
    comb = (jnp.where(lane == i1 - N_EXPERT_GROUPS, w1, 0.0)
            + jnp.where(lane == i2 - N_EXPERT_GROUPS, w2, 0.0))
    return h, comb


def _wo_kernel(o_ref, x_ref, wo_ref, x1_ref):
    x1_ref[...] = x_ref[...] + jnp.dot(o_ref[...], wo_ref[...], preferred_element_type=F32)


def _wo(o, x, wo, tm):
    n = x.shape[0]
    row = lambda i: (i, 0)
    return pl.pallas_call(
        _wo_kernel,
        grid=(n // tm,),
        in_specs=[pl.BlockSpec((tm, D_MODEL), row), pl.BlockSpec((tm, D_MODEL), row),
                  pl.BlockSpec((D_MODEL, D_MODEL), lambda i: (0, 0))],
        out_specs=pl.BlockSpec((tm, D_MODEL), row),
        out_shape=jax.ShapeDtypeStruct((n, D_MODEL), F32),
        compiler_params=_cparams("arbitrary"),
        name="attn_out_proj",
    )(o, x, wo)


def _moe_kernel(x_ref, nfw_ref, wr_ref, br_ref, wg_ref, wu_ref, wd_ref, nw_ref, y_ref,
                h_ref, c_ref, acc_ref, *, final_norm, in_rows, out_rows):
    g = pl.program_id(1)
    nlb = D_MODEL // LANES

    @pl.when(g == 0)
    def _():
        if in_rows:
            for r in range(acc_ref.shape[0] // SUBLANES):
                for b in range(nlb):
                    acc_ref[r * SUBLANES:(r + 1) * SUBLANES, b * LANES:(b + 1) * LANES] = (
                        x_ref[pl.ds(r * SUBLANES * nlb + b, SUBLANES, stride=nlb), :])
        else:
            acc_ref[...] = x_ref[...]
        h_ref[...], c_ref[...] = _route(acc_ref[...], nfw_ref[...], wr_ref[...], br_ref[...])

    h = h_ref[...]
    comb = c_ref[...]
    acc = acc_ref[...]
    for e in range(EXPERTS_PER_GROUP):
        gate = jnp.dot(h, wg_ref[e], preferred_element_type=F32)
        up = jnp.dot(h, wu_ref[e], preferred_element_type=F32)
        hid = (gate * jax.nn.sigmoid(gate) * up).astype(BF16)
        lane = lax.broadcasted_iota(jnp.int32, comb.shape, 1)
        ce = jnp.sum(jnp.where(lane == g * EXPERTS_PER_GROUP + e, comb, 0.0),
                     axis=-1, keepdims=True)
        acc = acc + ce * jnp.dot(hid, wd_ref[e], preferred_element_type=F32)
    acc_ref[...] = acc

    @pl.when(g == N_EXPERT_GROUPS - 1)
    def _():
        y = acc_ref[...]
        if final_norm:
            y = _rms(y, nw_ref[...])
        if out_rows:
            _store_token_head_rows(y_ref, y)
        else:
            y_ref[...] = y


def _moe(x, nfw, wr, br, wg, wu, wd, nw, final_norm, tm, in_rows=False, out_rows=False):
    nlb = D_MODEL // LANES
    n = x.shape[0] // nlb if in_rows else x.shape[0]
    row = lambda i, g: (i, 0)
    fix = lambda i, g: (0, 0)
    rows_spec = pl.BlockSpec((tm * nlb, LANES), row)
    nat_spec = pl.BlockSpec((tm, D_MODEL), row)
    kern = functools.partial(_moe_kernel, final_norm=final_norm, in_rows=in_rows,
                             out_rows=out_rows)
    return pl.pallas_call(
        kern,
        grid=(n // tm, N_EXPERT_GROUPS),
        in_specs=[rows_spec if in_rows else nat_spec, pl.BlockSpec((1, D_MODEL), fix),
                  pl.BlockSpec((D_MODEL, 2 * LANES), fix), pl.BlockSpec((1, LANES), fix),
                  pl.BlockSpec((EXPERTS_PER_GROUP, D_MODEL, D_EXPERT), lambda i, g: (g, 0, 0)),
                  pl.BlockSpec((EXPERTS_PER_GROUP, D_MODEL, D_EXPERT), lambda i, g: (g, 0, 0)),
                  pl.BlockSpec((EXPERTS_PER_GROUP, D_EXPERT, D_MODEL), lambda i, g: (g, 0, 0)),
                  pl.BlockSpec((1, D_MODEL), lambda i, g: (0, 0))],
        out_specs=rows_spec if out_rows else nat_spec,
        out_shape=jax.ShapeDtypeStruct((n * nlb, LANES) if out_rows else (n, D_MODEL), F32),
        scratch_shapes=[pltpu.VMEM((tm, D_MODEL), BF16), pltpu.VMEM((tm, LANES), F32),
                        pltpu.VMEM((tm, D_MODEL), F32)],
        compiler_params=_cparams("arbitrary", "arbitrary"),
        name="moe",
    )(x, nfw, wr, br, wg, wu, wd, nw)


def _cmul(ar, ai, br, bi):
    return ar * br - ai * bi, ar * bi + ai * br


def _s5_kernel(x_ref, x0r_ref, x0i_ref, nmw_ref, ar_ref, ai_ref, bm_ref, cm_ref, d_ref,
               wglu_ref, wgate_ref,
               x2_ref, sr_ref, si_ref,
               xp_ref, bu_all_ref, y_ref, pr_ref, pi_ref, cr_ref, ci_ref, car_r_ref, car_i_ref,
               *, chain):
    steps, rows, nst = SSM_STEPS, SSM_ROWS, SSM_BLK_ST

    @pl.when(pl.program_id(0) == 0)
    def _():
        for kb in range(SSM_NBLK):
            ar, ai = ar_ref[kb], ai_ref[kb]
            pr, pi = ar, ai
            for i in range(steps):
                if i:
                    pr, pi = _cmul(ar, ai, pr, pi)
                pr_ref[kb, i * SUBLANES:(i + 1) * SUBLANES, :] = jnp.broadcast_to(pr, (SUBLANES, nst))
                pi_ref[kb, i * SUBLANES:(i + 1) * SUBLANES, :] = jnp.broadcast_to(pi, (SUBLANES, nst))
        car_r_ref[...] = jnp.zeros_like(car_r_ref)
        car_i_ref[...] = jnp.zeros_like(car_i_ref)

    nlb = D_MODEL // LANES
    for i in range(steps):
        for b in range(nlb):
            xp_ref[i * SUBLANES:(i + 1) * SUBLANES, b * LANES:(b + 1) * LANES] = (
                x_ref[pl.ds(i * nlb + b, SUBLANES, stride=steps * nlb), :])

    u = _rms(xp_ref[...], nmw_ref[...])
    ub = u.astype(BF16)

    half = nst // 2
    for kb in range(SSM_NBLK):
        bu_ref = bu_all_ref.at[kb % bu_all_ref.shape[0]]
        bu_ref[...] = jnp.dot(ub[:, kb * SSM_BLK_CH:(kb + 1) * SSM_BLK_CH], bm_ref[kb],
                              preferred_element_type=F32)
        for hh in range(2):
            re = slice(hh * half, (hh + 1) * half)
            im = slice(nst + hh * half, nst + (hh + 1) * half)
            ar = jnp.broadcast_to(ar_ref[kb, :, re], (SUBLANES, half))
            ai = jnp.broadcast_to(ai_ref[kb, :, re], (SUBLANES, half))

            def scan_step(i, s, re=re, im=im, ar=ar, ai=ai, bu_ref=bu_ref):
                sr, si = s
                r0 = pl.multiple_of(i * SUBLANES, SUBLANES)
                tr, ti = _cmul(ar, ai, sr, si)
                sr = tr + bu_ref[pl.ds(r0, SUBLANES), re]
                si = ti + bu_ref[pl.ds(r0, SUBLANES), im]
                bu_ref[pl.ds(r0, SUBLANES), re] = sr
                bu_ref[pl.ds(r0, SUBLANES), im] = si
                return sr, si

            zero = jnp.zeros((SUBLANES, half), F32)
            lax.fori_loop(0, steps, scan_step, (zero, zero), unroll=True)

        if chain:
            alr, ali = pr_ref[kb, rows - 1:rows, :], pi_ref[kb, rows - 1:rows, :]
            cr, ci = car_r_ref[kb], car_i_ref[kb]
            for j in range(SUBLANES):
                cr_ref[j:j + 1, :] = cr
                ci_ref[j:j + 1, :] = ci
                last = rows - SUBLANES + j
                tr, ti = _cmul(alr, ali, cr, ci)
                cr = tr + bu_ref[last:last + 1, 0:nst]
                ci = ti + bu_ref[last:last + 1, nst:2 * nst]
            car_r_ref[kb] = cr
            car_i_ref[kb] = ci
        else:
            cr_ref[...] = x0r_ref[:, kb * nst:(kb + 1) * nst]
            ci_ref[...] = x0i_ref[:, kb * nst:(kb + 1) * nst]

        c_r, c_i = cr_ref[...], ci_ref[...]

        def fix_step(i, _, kb=kb, c_r=c_r, c_i=c_i, bu_ref=bu_ref):
            r0 = pl.multiple_of(i * SUBLANES, SUBLANES)
            tr, ti = _cmul(pr_ref[kb, pl.ds(r0, SUBLANES), :], pi_ref[kb, pl.ds(r0, SUBLANES), :],
                           c_r, c_i)
            bu_ref[pl.ds(r0, SUBLANES), 0:nst] = bu_ref[pl.ds(r0, SUBLANES), 0:nst] + tr
            bu_ref[pl.ds(r0, SUBLANES), nst:2 * nst] = bu_ref[pl.ds(r0, SUBLANES), nst:2 * nst] + ti
            return 0

        lax.fori_loop(0, steps, fix_step, 0, unroll=True)
        sr_ref[:, kb * nst:(kb + 1) * nst] = bu_ref[rows - SUBLANES:rows, 0:nst]
        si_ref[:, kb * nst:(kb + 1) * nst] = bu_ref[rows - SUBLANES:rows, nst:2 * nst]
        y_ref[:, kb * SSM_BLK_CH:(kb + 1) * SSM_BLK_CH] = jnp.dot(
            bu_ref[...].astype(BF16), cm_ref[kb], preferred_element_type=F32)

    y = y_ref[...] + d_ref[...] * u
    z = jax.nn.gelu(y).astype(BF16)
    out = (jnp.dot(z, wglu_ref[...], preferred_element_type=F32)
           * jax.nn.sigmoid(jnp.dot(z, wgate_ref[...], preferred_element_type=F32)))
    y_ref[...] = xp_ref[...] + out
    for i in range(steps):
        for b in range(nlb):
            x2_ref[pl.ds(i * nlb + b, SUBLANES, stride=steps * nlb), :] = (
                y_ref[i * SUBLANES:(i + 1) * SUBLANES, b * LANES:(b + 1) * LANES])


def _s5(x, x0r, x0i, nmw, a_re, a_im, bm, cm, d, wglu, wgate, chain):
    nlb = D_MODEL // LANES
    n = x.shape[0] // nlb
    nchunk = n // SSM_ROWS
    row = lambda i: (i, 0)
    fix2 = lambda i: (0, 0)
    fix3 = lambda i: (0, 0, 0)
    state_map = fix2 if chain else row
    n_state_rows = SUBLANES if chain else SUBLANES * nchunk
    kern = functools.partial(_s5_kernel, chain=chain)
    return pl.pallas_call(
        kern,
        grid=(nchunk,),
        in_specs=[pl.BlockSpec((SSM_ROWS * nlb, LANES), row),
                  pl.BlockSpec((SUBLANES, N_STATES), state_map),
                  pl.BlockSpec((SUBLANES, N_STATES), state_map),
                  pl.BlockSpec((1, D_MODEL), fix2),
                  pl.BlockSpec((SSM_NBLK, 1, SSM_BLK_ST), fix3),
                  pl.BlockSpec((SSM_NBLK, 1, SSM_BLK_ST), fix3),
                  pl.BlockSpec((SSM_NBLK, SSM_BLK_CH, 2 * SSM_BLK_ST), fix3),
                  pl.BlockSpec((SSM_NBLK, 2 * SSM_BLK_ST, SSM_BLK_CH), fix3),
                  pl.BlockSpec((1, D_MODEL), fix2),
                  pl.BlockSpec((D_MODEL, D_MODEL), fix2),
                  pl.BlockSpec((D_MODEL, D_MODEL), fix2)],
        out_specs=[pl.BlockSpec((SSM_ROWS * nlb, LANES), row),
                   pl.BlockSpec((SUBLANES, N_STATES), state_map),
                   pl.BlockSpec((SUBLANES, N_STATES), state_map)],
        out_shape=[jax.ShapeDtypeStruct((n * nlb, LANES), F32),
                   jax.ShapeDtypeStruct((n_state_rows, N_STATES), F32),
                   jax.ShapeDtypeStruct((n_state_rows, N_STATES), F32)],
        scratch_shapes=[pltpu.VMEM((SSM_ROWS, D_MODEL), F32),
                        pltpu.VMEM((1, SSM_ROWS, 2 * SSM_BLK_ST), F32),
                        pltpu.VMEM((SSM_ROWS, D_MODEL), F32),
                        pltpu.VMEM((SSM_NBLK, SSM_ROWS, SSM_BLK_ST), F32),
                        pltpu.VMEM((SSM_NBLK, SSM_ROWS, SSM_BLK_ST), F32),
                        pltpu.VMEM((SUBLANES, SSM_BLK_ST), F32),
                        pltpu.VMEM((SUBLANES, SSM_BLK_ST), F32),
                        pltpu.VMEM((SSM_NBLK, 1, SSM_BLK_ST), F32),
                        pltpu.VMEM((SSM_NBLK, 1, SSM_BLK_ST), F32)],
        compiler_params=_cparams("arbitrary"),
        name="s5_mixer",
    )(x, x0r, x0i, nmw, a_re, a_im, bm, cm, d, wglu, wgate)


def _rope_cos_sin(pos):
    half = ROT_DIM // 2
    inv_freq = ROPE_THETA ** (-jnp.arange(half, dtype=F32) * 2.0 / ROT_DIM)
    ang = pos.astype(F32)[:, None] * inv_freq[None, :]
    return jnp.cos(ang), jnp.sin(ang)


def _rope_expand_matrix():
    half = ROT_DIM // 2
    e = np.zeros((ROT_DIM, 3 * LANES), np.float32)
    for lane in range(LANES):
        d = lane % HEAD_DIM
        if d < ROT_DIM:
            e[d % half, lane] = 1.0
        if d < half:
            e[half + d, LANES + lane] = -1.0
        elif d < ROT_DIM:
            e[half + d - half, 2 * LANES + lane] = 1.0
    return jnp.asarray(e)


def _rope_tables(pos):
    half = ROT_DIM // 2
    cos, sin = _rope_cos_sin(pos)
    n = pos.shape[0]
    pad = HEAD_DIM - ROT_DIM
    cosf = jnp.concatenate([cos, cos, jnp.ones((n, pad), F32)], axis=1)
    sina = jnp.concatenate([-sin, jnp.zeros((n, half + pad), F32)], axis=1)
    sinb = jnp.concatenate([jnp.zeros((n, half), F32), sin, jnp.zeros((n, pad), F32)], axis=1)
    tile2 = lambda t: jnp.concatenate([t, t], axis=1)
    return tile2(cosf), tile2(sina), tile2(sinb)


def _s5_params(lam_re, lam_im, log_dt, b_re, b_im, c_re, c_im):
    dt = jnp.exp(log_dt)[:, None]
    z_re, z_im = lam_re * dt, lam_im * dt
    mag = jnp.exp(z_re)
    lb_re, lb_im = mag * jnp.cos(z_im), mag * jnp.sin(z_im)
    n_re, n_im = lb_re - 1.0, lb_im
    den = lam_re * lam_re + lam_im * lam_im
    k_re = (n_re * lam_re + n_im * lam_im) / den
    k_im = (n_im * lam_re - n_re * lam_im) / den
    bb_re = k_re[..., None] * b_re - k_im[..., None] * b_im
    bb_im = k_re[..., None] * b_im + k_im[..., None] * b_re
    gpb = SSM_BLK_CH // SSM_GROUP
    eye = jnp.eye(gpb, dtype=F32)

    def in_blocks(bb):
        v = bb.reshape(SSM_NBLK, gpb, SSM_STATE, SSM_GROUP).transpose(0, 1, 3, 2)
        return jnp.einsum('kgcp,gh->kgchp', v, eye).reshape(SSM_NBLK, SSM_BLK_CH, SSM_BLK_ST)

    def out_blocks(cc):
        v = cc.reshape(SSM_NBLK, gpb, SSM_GROUP, SSM_STATE).transpose(0, 1, 3, 2)
        return jnp.einsum('kgpc,gh->kgphc', v, eye).reshape(SSM_NBLK, SSM_BLK_ST, SSM_BLK_CH)

    bm = jnp.concatenate([in_blocks(bb_re), in_blocks(bb_im)], axis=2).astype(BF16)
    cm = jnp.concatenate([out_blocks(c_re), -out_blocks(c_im)], axis=1).astype(BF16)
    a_re = lb_re.reshape(SSM_NBLK, 1, SSM_BLK_ST)
    a_im = lb_im.reshape(SSM_NBLK, 1, SSM_BLK_ST)
    return a_re, a_im, bm, cm


def _router_params(w_rg, b_rg, w_re, b_re):
    pad = LANES - N_EXPERT_GROUPS - N_EXPERTS
    wr = jnp.concatenate([w_rg, w_re, jnp.zeros((D_MODEL, pad), F32)], axis=1)
    w_hi = wr.astype(BF16)
    wr = jnp.concatenate([w_hi, (wr - w_hi.astype(F32)).astype(BF16)], axis=1)
    br = jnp.concatenate([b_rg, b_re, jnp.zeros((pad,), F32)])[None, :]
    return wr, br


def kernel(x_prompt, x_sample, cache_k, cache_v, state_ssm_re, state_ssm_im, norm_mix, norm_ffn, norm_final, attn_w_qkv, attn_lambda_q1, attn_lambda_k1, attn_lambda_q2, attn_lambda_k2, attn_subln, attn_w_o, ssm_lambda_re, ssm_lambda_im, ssm_log_dt, ssm_b_re, ssm_b_im, ssm_c_re, ssm_c_im, ssm_d, ssm_w_glu, ssm_w_gate, moe_w_router_group, moe_b_router_group, moe_w_router_expert, moe_b_router_expert, moe_w_gate, moe_w_up, moe_w_down):
    _, seq, _ = x_prompt.shape
    nb, dec, _ = x_sample.shape
    past = cache_k.shape[2]
    xp = x_prompt.reshape(seq, D_MODEL)
    xs = x_sample.reshape(nb * dec, D_MODEL)

    lam_init = 0.8 - 0.6 * math.exp(-0.3 * 0)
    lam = (jnp.exp(jnp.sum(attn_lambda_q1[0] * attn_lambda_k1[0]))
           - jnp.exp(jnp.sum(attn_lambda_q2[0] * attn_lambda_k2[0])) + lam_init)
    lam_p = jnp.full((1, ATT_T), lam, F32)
    lam_s = jnp.full((1, V_DIM), lam, F32)
    w_subln = attn_subln[0][None, :]
    out_scale = 1.0 - lam_init
    wqkv = attn_w_qkv[0].astype(BF16)
    wq, wk, wv = (wqkv[:, j * D_MODEL:(j + 1) * D_MODEL] for j in range(3))
    wo = attn_w_o[0].astype(BF16)
    nm0 = norm_mix[0][None, :]
    cos_p, sin_p = _rope_cos_sin(jnp.arange(seq, dtype=jnp.int32))
    tabs_s = tuple(jnp.tile(t, (nb, 1))
                   for t in _rope_tables(past + jnp.arange(dec, dtype=jnp.int32)))

    k_p, v_p, kb_p, qt_p, vt_p = _qkv_prompt(
        xp, nm0, wq.T, wk, wv, wv.T, jnp.concatenate([cos_p, sin_p], axis=1),
        _rope_expand_matrix(), cos_p.T, sin_p.T, tm=512)
    q_s, k_s, v_s, kb_s, vb_s = _qkv(xs, nm0, wqkv, *tabs_s, tm=512)
    o_p = _attn_prompt(qt_p, kb_p, vt_p, w_subln, lam_p, out_scale)
    o_s = _attn_sample(q_s, kb_s, vb_s, cache_k[0].reshape(nb, past * N_HEADS, V_DIM),
                       cache_v[0].reshape(nb, past * N_HEADS, V_DIM), w_subln, lam_s, out_scale,
                       t=dec)

    nf0 = norm_ffn[0][None, :]
    wr0, br0 = _router_params(moe_w_router_group[0], moe_b_router_group[0],
                              moe_w_router_expert[0], moe_b_router_expert[0])
    x1_p = _wo(o_p, xp, wo, tm=512)
    x1_s = _wo(o_s, xs, wo, tm=512)
    wg0, wu0, wd0 = (moe_w_gate[0].astype(BF16), moe_w_up[0].astype(BF16),
                     moe_w_down[0].astype(BF16))
    nfin = norm_final[None, :]
    xp1 = _moe(x1_p, nf0, wr0, br0, wg0, wu0, wd0, nfin, False, tm=1024, out_rows=True)
    xs1 = _moe(x1_s, nf0, wr0, br0, wg0, wu0, wd0, nfin, False, tm=512, out_rows=True)

    a_re, a_im, bm, cm = _s5_params(ssm_lambda_re[0], ssm_lambda_im[0], ssm_log_dt[0],
                                    ssm_b_re[0], ssm_b_im[0], ssm_c_re[0], ssm_c_im[0])
    nm1 = norm_mix[1][None, :]
    nf1 = norm_ffn[1][None, :]
    wr1, br1 = _router_params(moe_w_router_group[1], moe_b_router_group[1],
                              moe_w_router_expert[1], moe_b_router_expert[1])
    s5w = (nm1, a_re, a_im, bm, cm, ssm_d[0][None, :], ssm_w_glu[0].astype(BF16),
           ssm_w_gate[0].astype(BF16))
    zero_state = jnp.zeros((SUBLANES, N_STATES), F32)
    x2_p, sr_p, si_p = _s5(xp1, zero_state, zero_state, *s5w, chain=True)
    x0r = state_ssm_re[0].reshape(nb, N_STATES)
    x0i = state_ssm_im[0].reshape(nb, N_STATES)
    x2_s, sr_s, si_s = _s5(xs1, x0r, x0i, *s5w, chain=False)

    wg1, wu1, wd1 = (moe_w_gate[1].astype(BF16), moe_w_up[1].astype(BF16),
                     moe_w_down[1].astype(BF16))
    y_p = _moe(x2_p, nf1, wr1, br1, wg1, wu1, wd1, nfin, True, tm=1024, in_rows=True)
    y_s = _moe(x2_s, nf1, wr1, br1, wg1, wu1, wd1, nfin, True, tm=512, in_rows=True)

    hshape = (N_HEADS, V_DIM)
    gshape = (N_SSM_GROUPS, SSM_STATE)
    return (y_p.reshape(1, seq, D_MODEL), y_s.reshape(nb, dec, D_MODEL),
            k_p.reshape(1, 1, seq, *hshape), v_p.reshape(1, 1, seq, *hshape),
            sr_p[SUBLANES - 1].reshape(1, 1, *gshape), si_p[SUBLANES - 1].reshape(1, 1, *gshape),
            k_s.reshape(1, nb, dec, *hshape), v_s.reshape(1, nb, dec, *hshape),
            sr_s.reshape(1, nb, *gshape), si_s.reshape(1, nb, *gshape))
```

```python
import functools
import math

import jax
import jax.numpy as jnp
import numpy as np
from jax import lax
from jax.experimental import pallas as pl
from jax.experimental.pallas import tpu as pltpu

F32 = jnp.float32
BF16 = jnp.bfloat16

D_MODEL = 1024
CHUNK = 64
N_HEADS = 8
HEAD_DIM = 64
V_DIM = 2 * HEAD_DIM
ROT_DIM = HEAD_DIM // 4
ROPE_THETA = 500000.0
SSM_GROUP = 16
N_SSM_GROUPS = D_MODEL // SSM_GROUP
SSM_STATE = 64
N_STATES = N_SSM_GROUPS * SSM_STATE
N_EXPERT_GROUPS = 4
EXPERTS_PER_GROUP = 4
N_EXPERTS = N_EXPERT_GROUPS * EXPERTS_PER_GROUP
D_EXPERT = D_MODEL // 4
RMS_EPS = 1e-6

LANES = 128
SUBLANES = 8
VMEM_LIMIT = 56 * 1024 * 1024

SSM_BLK_CH = 256
SSM_NBLK = D_MODEL // SSM_BLK_CH
SSM_BLK_ST = (SSM_BLK_CH // SSM_GROUP) * SSM_STATE
SSM_STEPS = 32
SSM_ROWS = SUBLANES * SSM_STEPS


def _cparams(*sem, flags=None):
    return pltpu.CompilerParams(dimension_semantics=sem, vmem_limit_bytes=VMEM_LIMIT, flags=flags)


def _rms(x, w):
    return x * lax.rsqrt(jnp.mean(x * x, axis=-1, keepdims=True) + RMS_EPS) * w


def _qkv_kernel(x_ref, nw_ref, w_ref, cos_ref, sina_ref, sinb_ref,
                q_ref, k_ref, v_ref, kb_ref, vb_ref):
    h = _rms(x_ref[...], nw_ref[...]).astype(BF16)
    cosf, sina, sinb = cos_ref[...], sina_ref[...], sinb_ref[...]

    def rope(t):
        outs = []
        for hh in range(N_HEADS):
            xs = t[:, hh * LANES:(hh + 1) * LANES]
            outs.append(xs * cosf + pltpu.roll(xs, LANES - ROT_DIM // 2, 1) * sina
                        + pltpu.roll(xs, ROT_DIM // 2, 1) * sinb)
        return jnp.concatenate(outs, axis=1)

    q = rope(jnp.dot(h, w_ref[:, 0:D_MODEL], preferred_element_type=F32))
    q_ref[...] = (q * (HEAD_DIM ** -0.5)).astype(BF16)
    k = rope(jnp.dot(h, w_ref[:, D_MODEL:2 * D_MODEL], preferred_element_type=F32))
    k_ref[...] = k
    kb_ref[...] = k.astype(BF16)
    v = jnp.dot(h, w_ref[:, 2 * D_MODEL:3 * D_MODEL], preferred_element_type=F32)
    v_ref[...] = v
    vb_ref[...] = v.astype(BF16)


def _qkv(x, nw, w, cosf, sina, sinb, tm):
    n = x.shape[0]
    row = lambda i: (i, 0)
    fix = lambda i: (0, 0)
    return pl.pallas_call(
        _qkv_kernel,
        grid=(n // tm,),
        in_specs=[pl.BlockSpec((tm, D_MODEL), row), pl.BlockSpec((1, D_MODEL), fix),
                  pl.BlockSpec((D_MODEL, 3 * D_MODEL), fix),
                  pl.BlockSpec((tm, LANES), row), pl.BlockSpec((tm, LANES), row),
                  pl.BlockSpec((tm, LANES), row)],
        out_specs=[pl.BlockSpec((tm, D_MODEL), row)] * 5,
        out_shape=[jax.ShapeDtypeStruct((n, D_MODEL), BF16),
                   jax.ShapeDtypeStruct((n, D_MODEL), F32),
                   jax.ShapeDtypeStruct((n, D_MODEL), F32),
                   jax.ShapeDtypeStruct((n, D_MODEL), BF16),
                   jax.ShapeDtypeStruct((n, D_MODEL), BF16)],
        compiler_params=_cparams("arbitrary"),
        name="qkv_rope",
    )(x, nw, w, cosf, sina, sinb)


ATT_T = 512
ATT_K = 256
ATT_ONES = 16
ATT_UNROLL = 4
ATT_HEADS = 2
ATT_STRIP = 256


def _store_token_head_rows(ref, x):
    for r in range(x.shape[0] // SUBLANES):
        for hh in range(N_HEADS):
            ref[pl.ds(r * SUBLANES * N_HEADS + hh, SUBLANES, stride=N_HEADS), :] = (
                x[r * SUBLANES:(r + 1) * SUBLANES, hh * LANES:(hh + 1) * LANES])


def _qkv_prompt_kernel(x_ref, nw_ref, wqt_ref, wk_ref, wv_ref,
                       cs_ref, expand_ref, cost_ref, sint_ref,
                       k_ref, v_ref, kb_ref, qt_ref, vt_ref):
    h = _rms(x_ref[...], nw_ref[...]).astype(BF16)
    tabs = jnp.dot(cs_ref[...], expand_ref[...], precision=lax.Precision.HIGHEST,
                   preferred_element_type=F32)
    lane = lax.broadcasted_iota(jnp.int32, (1, LANES), 1)
    cosf = tabs[:, 0:LANES] + jnp.where(lane % HEAD_DIM >= ROT_DIM, 1.0, 0.0)
    sina, sinb = tabs[:, LANES:2 * LANES], tabs[:, 2 * LANES:3 * LANES]
    outs = []
    k = jnp.dot(h, wk_ref[...], preferred_element_type=F32)
    for hh in range(N_HEADS):
        xs = k[:, hh * LANES:(hh + 1) * LANES]
        outs.append(xs * cosf + pltpu.roll(xs, LANES - ROT_DIM // 2, 1) * sina
                    + pltpu.roll(xs, ROT_DIM // 2, 1) * sinb)
    k = jnp.concatenate(outs, axis=1)
    _store_token_head_rows(k_ref, k)
    kb_ref[...] = k.astype(BF16)
    v = jnp.dot(h, wv_ref[...], preferred_element_type=F32)
    _store_token_head_rows(v_ref, v)
    vt = v.T.astype(BF16)

    nt = (((1,), (1,)), ((), ()))
    qt = lax.dot_general(wqt_ref[...], h, nt, preferred_element_type=F32) * (
        HEAD_DIM ** -0.5 * math.log2(math.e))
    cost, sint = cost_ref[...], sint_ref[...]
    half = ROT_DIM // 2
    pieces = []
    for blk in range(D_MODEL // HEAD_DIM):
        r0 = blk * HEAD_DIM
        x1, x2 = qt[r0:r0 + half], qt[r0 + half:r0 + ROT_DIM]
        pieces += [x1 * cost - x2 * sint, x2 * cost + x1 * sint, qt[r0 + ROT_DIM:r0 + HEAD_DIM]]
    qt = jnp.concatenate(pieces, axis=0).astype(BF16)
    for b in range(qt_ref.shape[0]):
        qt_ref[b] = qt[:, b * ATT_T:(b + 1) * ATT_T]
    for b in range(vt_ref.shape[0]):
        vt_ref[b] = vt[:, b * ATT_K:(b + 1) * ATT_K]


def _qkv_prompt(x, nw, wqt, wk, wv, cs, expand, cost, sint, tm):
    n = x.shape[0]
    row = lambda i: (i, 0)
    fix = lambda i: (0, 0)
    wspec = pl.BlockSpec((D_MODEL, D_MODEL), fix)
    ttspec = pl.BlockSpec((ROT_DIM // 2, tm), lambda i: (0, i))
    thspec = pl.BlockSpec((tm * N_HEADS, V_DIM), row)
    qblk = pl.BlockSpec((tm // ATT_T, D_MODEL, ATT_T), lambda i: (i, 0, 0))
    vblk = pl.BlockSpec((tm // ATT_K, D_MODEL, ATT_K), lambda i: (i, 0, 0))
    return pl.pallas_call(
        _qkv_prompt_kernel,
        grid=(n // tm,),
        in_specs=[pl.BlockSpec((tm, D_MODEL), row), pl.BlockSpec((1, D_MODEL), fix),
                  wspec, wspec, wspec,
                  pl.BlockSpec((tm, ROT_DIM), row), pl.BlockSpec((ROT_DIM, 3 * LANES), fix),
                  ttspec, ttspec],
        out_specs=[thspec, thspec, pl.BlockSpec((tm, D_MODEL), row), qblk, vblk],
        out_shape=[jax.ShapeDtypeStruct((n * N_HEADS, V_DIM), F32),
                   jax.ShapeDtypeStruct((n * N_HEADS, V_DIM), F32),
                   jax.ShapeDtypeStruct((n, D_MODEL), BF16),
                   jax.ShapeDtypeStruct((n // ATT_T, D_MODEL, ATT_T), BF16),
                   jax.ShapeDtypeStruct((n // ATT_K, D_MODEL, ATT_K), BF16)],
        compiler_params=_cparams("arbitrary"),
        name="qkv_rope_prompt",
    )(x, nw, wqt, wk, wv, cs, expand, cost, sint)


def _stack_subheads(q):
    lane = lax.broadcasted_iota(jnp.int32, q.shape, 1)
    zero = jnp.zeros_like(q)
    return jnp.concatenate([jnp.where(lane < HEAD_DIM, q, zero),
                            jnp.where(lane >= HEAD_DIM, q, zero)], axis=0)


def _scores(qs, k):
    return lax.dot_general(qs, k, (((1,), (1,)), ((), ())), preferred_element_type=F32)


def _diff_finish(acc, l, t, lam, w, out_scale):
    o = acc[:t] / l[:t] - lam * (acc[t:] / l[t:])
    return _rms(o, w) * out_scale


def _attn_prompt_kernel(qt_ref, k_ref, vt_ref, w_ref, lam_ref, o_ref, sa_ref, sb_ref, acc_ref,
                        bias_ref, *, out_scale):
    t, tk = ATT_T, ATT_K
    heads = range(ATT_HEADS)
    i = pl.program_id(1)
    row = lax.broadcasted_iota(jnp.int32, (V_DIM, t), 0)
    qs = []
    for g in heads:
        qt = qt_ref[0, g * V_DIM:(g + 1) * V_DIM, :]
        zero = jnp.zeros_like(qt)
        qs.append(jnp.concatenate([jnp.where(row < HEAD_DIM, qt, zero),
                                   jnp.where(row >= HEAD_DIM, qt, zero)], axis=1))

    def scores(ref, g, j, bias=None):
        r0 = pl.multiple_of(j * tk, tk)
        kblk = k_ref[pl.ds(r0, tk), g * V_DIM:(g + 1) * V_DIM]
        smax = []
        for c in range(2 * t // ATT_STRIP):
            cols = slice(c * ATT_STRIP, (c + 1) * ATT_STRIP)
            s = jnp.dot(kblk, qs[g][:, cols], preferred_element_type=F32)
            if bias is not None:
                s = s + bias[:, cols]
            ref[g, :, cols] = s
            smax.append(jnp.max(s, axis=0, keepdims=True))
        return jnp.concatenate(smax, axis=1)

    ones = jnp.ones((ATT_ONES, tk), BF16)

    def update(g, m, ref, smax, j):
        m_new = jnp.maximum(m, smax)
        alpha = jnp.exp2(m - m_new)
        v1 = jnp.concatenate([vt_ref[j, g * V_DIM:(g + 1) * V_DIM, :], ones], axis=0)
        for c in range(2 * t // ATT_STRIP):
            cols = slice(c * ATT_STRIP, (c + 1) * ATT_STRIP)
            p = jnp.exp2(ref[g, :, cols] - m_new[:, cols]).astype(BF16)
            acc_ref[g, :, cols] = (alpha[:, cols] * acc_ref[g, :, cols]
                                   + jnp.dot(v1, p, preferred_element_type=F32))
        return m_new

    @pl.when((pl.program_id(0) == 0) & (i == 0))
    def _():
        for b in range(t // tk):
            kc = (b * tk + lax.broadcasted_iota(jnp.int32, (tk, 2 * t), 0)) // CHUNK
            qc = (lax.broadcasted_iota(jnp.int32, (tk, 2 * t), 1) % t) // CHUNK
            bias_ref[b] = jnp.where(kc <= qc, 0.0, -jnp.inf)

    acc_ref[...] = jnp.zeros(acc_ref.shape, F32)
    first_bias = jnp.where(i == 0, bias_ref[0], 0.0)
    xa = tuple(scores(sa_ref, g, 0, first_bias) for g in heads)

    def pairs(n, last_bias=None):
        def body(jj, carry):
            ms, xa = list(carry[0]), list(carry[1])
            for u in range(n):
                j = 2 * (jj * n + u)
                xb = [scores(sb_ref, g, j + 1) for g in heads]
                for g in heads:
                    ms[g] = update(g, ms[g], sa_ref, xa[g], j)
                for g in heads:
                    xa[g] = scores(sa_ref, g, j + 2, last_bias)
                for g in heads:
                    ms[g] = update(g, ms[g], sb_ref, xb[g], j + 1)
            return tuple(ms), tuple(xa)
        return body

    ms = tuple(jnp.full((1, 2 * t), -jnp.inf, F32) for _ in heads)
    nfree = jnp.maximum(i - 1, 0)
    nlong = nfree // ATT_UNROLL
    carry = lax.fori_loop(0, nlong, pairs(ATT_UNROLL), (ms, xa))
    carry = lax.fori_loop(nlong * ATT_UNROLL, nfree, pairs(1), carry)
    ms, xa = lax.fori_loop(nfree, i, pairs(1, bias_ref[0]), carry)

    xb = [scores(sb_ref, g, 2 * i + 1, bias_ref[1]) for g in heads]
    ms = [update(g, ms[g], sa_ref, xa[g], 2 * i) for g in heads]
    for g in heads:
        update(g, ms[g], sb_ref, xb[g], 2 * i + 1)

    for g in heads:
        acc = acc_ref[g, 0:V_DIM, :]
        l = acc_ref[g, V_DIM:V_DIM + 1, :]
        o = acc[:, :t] / l[:, :t] - lam_ref[...] * (acc[:, t:] / l[:, t:])
        o = o * lax.rsqrt(jnp.mean(o * o, axis=0, keepdims=True) + RMS_EPS)
        o_ref[:, g * V_DIM:(g + 1) * V_DIM] = (o.T * w_ref[...] * out_scale).astype(BF16)


def _attn_prompt(qt, kb, vt, w_subln, lam, out_scale):
    l = kb.shape[0]
    t, tk, ng = ATT_T, ATT_K, ATT_HEADS
    kern = functools.partial(_attn_prompt_kernel, out_scale=out_scale)
    return pl.pallas_call(
        kern,
        grid=(N_HEADS // ng, l // t),
        in_specs=[pl.BlockSpec((1, ng * V_DIM, t), lambda h, i: (i, h, 0)),
                  pl.BlockSpec((l, ng * V_DIM), lambda h, i: (0, h)),
                  pl.BlockSpec((l // tk, ng * V_DIM, tk), lambda h, i: (0, h, 0)),
                  pl.BlockSpec((1, V_DIM), lambda h, i: (0, 0)),
                  pl.BlockSpec((1, t), lambda h, i: (0, 0))],
        out_specs=pl.BlockSpec((t, ng * V_DIM), lambda h, i: (i, h)),
        out_shape=jax.ShapeDtypeStruct((l, D_MODEL), BF16),
        scratch_shapes=[pltpu.VMEM((ng, tk, 2 * t), F32), pltpu.VMEM((ng, tk, 2 * t), F32),
                        pltpu.VMEM((ng, V_DIM + ATT_ONES, 2 * t), F32),
                        pltpu.VMEM((t // tk, tk, 2 * t), F32)],
        compiler_params=_cparams("arbitrary", "arbitrary"),
        name="attn_prompt",
    )(qt, kb, vt, w_subln, lam)


def _attn_sample_kernel(q_ref, kn_ref, vn_ref, ck_ref, cv_ref, w_ref, lam_ref, o_ref,
                        *, t, out_scale):
    for hh in range(N_HEADS):
        cols = slice(hh * V_DIM, (hh + 1) * V_DIM)
        qs = _stack_subheads(q_ref[:, cols])
        past = ck_ref.shape[1] // N_HEADS
        kc = ck_ref[0, pl.ds(hh, past, stride=N_HEADS), :].astype(BF16)
        vc = cv_ref[0, pl.ds(hh, past, stride=N_HEADS), :].astype(BF16)
        s_c = _scores(qs, kc)
        s_n = _scores(qs, kn_ref[:, cols])
        m = jnp.maximum(jnp.max(s_c, axis=-1, keepdims=True), jnp.max(s_n, axis=-1, keepdims=True))
        p_c = jnp.exp(s_c - m)
        p_n = jnp.exp(s_n - m)
        l = jnp.sum(p_c, axis=-1, keepdims=True) + jnp.sum(p_n, axis=-1, keepdims=True)
        acc = (jnp.dot(p_c.astype(BF16), vc, preferred_element_type=F32)
               + jnp.dot(p_n.astype(BF16), vn_ref[:, cols], preferred_element_type=F32))
        o_ref[:, cols] = _diff_finish(acc, l, t, lam_ref[...], w_ref[...], out_scale).astype(BF16)


def _attn_sample(q, kb, vb, cache_k, cache_v, w_subln, lam, out_scale, t):
    nb, rows, _ = cache_k.shape
    kern = functools.partial(_attn_sample_kernel, t=t, out_scale=out_scale)
    row = lambda b: (b, 0)
    return pl.pallas_call(
        kern,
        grid=(nb,),
        in_specs=[pl.BlockSpec((t, D_MODEL), row), pl.BlockSpec((t, D_MODEL), row),
                  pl.BlockSpec((t, D_MODEL), row),
                  pl.BlockSpec((1, rows, V_DIM), lambda b: (b, 0, 0)),
                  pl.BlockSpec((1, rows, V_DIM), lambda b: (b, 0, 0)),
                  pl.BlockSpec((1, V_DIM), lambda b: (0, 0)),
                  pl.BlockSpec((1, V_DIM), lambda b: (0, 0))],
        out_specs=pl.BlockSpec((t, D_MODEL), row),
        out_shape=jax.ShapeDtypeStruct((nb * t, D_MODEL), BF16),
        compiler_params=_cparams("arbitrary"),
        name="attn_sample",
    )(q, kb, vb, cache_k, cache_v, w_subln, lam)


def _route(x, nw, wr, br):
    hf = _rms(x, nw)
    h = hf.astype(BF16)
    h_lo = (hf - h.astype(F32)).astype(BF16)
    hw = jnp.dot(h, wr, preferred_element_type=F32)
    lg = (hw[:, :LANES] + hw[:, LANES:] + jnp.dot(h_lo, wr[:, :LANES], preferred_element_type=F32)
          + br)
    lane = lax.broadcasted_iota(jnp.int32, lg.shape, 1)
    big = jnp.int32(LANES)
    neg = jnp.float32(-jnp.inf)

    def masked_softmax(mask):
        z = jnp.where(mask, lg, neg)
        e = jnp.exp(z - jnp.max(z, axis=-1, keepdims=True))
        return e / jnp.sum(e, axis=-1, keepdims=True)

    def top1(p, mask):
        v = jnp.max(jnp.where(mask, p, -1.0), axis=-1, keepdims=True)
        idx = jnp.min(jnp.where(mask & (p == v), lane, big), axis=-1, keepdims=True)
        return v, idx

    gmask = lane < N_EXPERT_GROUPS
    g_val, g_idx = top1(masked_softmax(gmask), gmask)
    lo = N_EXPERT_GROUPS + g_idx * EXPERTS_PER_GROUP
    emask = (lane >= lo) & (lane < lo + EXPERTS_PER_GROUP)
    pe = masked_softmax(emask)
    v1, i1 = top1(pe, emask)
    v2, i2 = top1(pe, emask & (lane != i1))
    den = v1 + v2
    w1 = g_val * v1 / den
    w2 = g_val * v2 / den
    comb = (jnp.where(lane == i1 - N_EXPERT_GROUPS, w1, 0.0)
            + jnp.where(lane == i2 - N_EXPERT_GROUPS, w2, 0.0))
    return h, comb


def _moe_kernel(*refs, final_norm, in_rows, out_rows, attn_proj):
    if attn_proj:
        x_ref, o_ref, wo_ref = refs[:3]
        refs = refs[3:]
    else:
        x_ref, refs = refs[0], refs[1:]
    (nfw_ref, wr_ref, br_ref, wg_ref, wu_ref, wd_ref, nw_ref, y_ref,
     h_ref, c_ref, acc_ref) = refs
    g = pl.program_id(1)
    nlb = D_MODEL // LANES

    @pl.when(g == 0)
    def _():
        if in_rows:
            for r in range(acc_ref.shape[0] // SUBLANES):
                for b in range(nlb):
                    acc_ref[r * SUBLANES:(r + 1) * SUBLANES, b * LANES:(b + 1) * LANES] = (
                        x_ref[pl.ds(r * SUBLANES * nlb + b, SUBLANES, stride=nlb), :])
        elif attn_proj:
            acc_ref[...] = x_ref[...] + jnp.dot(o_ref[...], wo_ref[...],
                                                preferred_element_type=F32)
        else:
            acc_ref[...] = x_ref[...]
        h_ref[...], c_ref[...] = _route(acc_ref[...], nfw_ref[...], wr_ref[...], br_ref[...])

    h = h_ref[...]
    comb = c_ref[...]
    acc = acc_ref[...]
    for e in range(EXPERTS_PER_GROUP):
        gate = jnp.dot(h, wg_ref[e], preferred_element_type=F32)
        up = jnp.dot(h, wu_ref[e], preferred_element_type=F32)
        hid = (gate * jax.nn.sigmoid(gate) * up).astype(BF16)
        lane = lax.broadcasted_iota(jnp.int32, comb.shape, 1)
        ce = jnp.sum(jnp.where(lane == g * EXPERTS_PER_GROUP + e, comb, 0.0),
                     axis=-1, keepdims=True)
        acc = acc + ce * jnp.dot(hid, wd_ref[e], preferred_element_type=F32)
    acc_ref[...] = acc

    @pl.when(g == N_EXPERT_GROUPS - 1)
    def _():
        y = acc_ref[...]
        if final_norm:
            y = _rms(y, nw_ref[...])
        if out_rows:
            _store_token_head_rows(y_ref, y)
        else:
            y_ref[...] = y


def _moe(x, nfw, wr, br, wg, wu, wd, nw, final_norm, tm, in_rows=False, out_rows=False,
         attn_proj=None):
    nlb = D_MODEL // LANES
    n = x.shape[0] // nlb if in_rows else x.shape[0]
    row = lambda i, g: (i, 0)
    fix = lambda i, g: (0, 0)
    rows_spec = pl.BlockSpec((tm * nlb, LANES), row)
    nat_spec = pl.BlockSpec((tm, D_MODEL), row)
    kern = functools.partial(_moe_kernel, final_norm=final_norm, in_rows=in_rows,
                             out_rows=out_rows, attn_proj=attn_proj is not None)
    proj_specs = [nat_spec, pl.BlockSpec((D_MODEL, D_MODEL), fix)] if attn_proj else []
    return pl.pallas_call(
        kern,
        grid=(n // tm, N_EXPERT_GROUPS),
        in_specs=[rows_spec if in_rows else nat_spec] + proj_specs + [
                  pl.BlockSpec((1, D_MODEL), fix),
                  pl.BlockSpec((D_MODEL, 2 * LANES), fix), pl.BlockSpec((1, LANES), fix),
                  pl.BlockSpec((EXPERTS_PER_GROUP, D_MODEL, D_EXPERT), lambda i, g: (g, 0, 0)),
                  pl.BlockSpec((EXPERTS_PER_GROUP, D_MODEL, D_EXPERT), lambda i, g: (g, 0, 0)),
                  pl.BlockSpec((EXPERTS_PER_GROUP, D_EXPERT, D_MODEL), lambda i, g: (g, 0, 0)),
                  pl.BlockSpec((1, D_MODEL), lambda i, g: (0, 0))],
        out_specs=rows_spec if out_rows else nat_spec,
        out_shape=jax.ShapeDtypeStruct((n * nlb, LANES) if out_rows else (n, D_MODEL), F32),
        scratch_shapes=[pltpu.VMEM((tm, D_MODEL), BF16), pltpu.VMEM((tm, LANES), F32),
                        pltpu.VMEM((tm, D_MODEL), F32)],
        compiler_params=_cparams("arbitrary", "arbitrary"),
        name="moe",
    )(x, *(attn_proj or ()), nfw, wr, br, wg, wu, wd, nw)


def _cmul(ar, ai, br, bi):
    return ar * br - ai * bi, ar * bi + ai * br


def _s5_kernel(x_ref, x0r_ref, x0i_ref, nmw_ref, ar_ref, ai_ref, bm_ref, cm_ref, d_ref,
               wglu_ref, wgate_ref,
               x2_ref, sr_ref, si_ref,
               xp_ref, bu_all_ref, y_ref, pr_ref, pi_ref, cr_ref, ci_ref, car_r_ref, car_i_ref,
               *, chain):
    steps, rows, nst = SSM_STEPS, SSM_ROWS, SSM_BLK_ST

    @pl.when(pl.program_id(0) == 0)
    def _():
        for kb in range(SSM_NBLK):
            ar, ai = ar_ref[kb], ai_ref[kb]
            pr, pi = ar, ai
            for i in range(steps):
                if i:
                    pr, pi = _cmul(ar, ai, pr, pi)
                pr_ref[kb, i * SUBLANES:(i + 1) * SUBLANES, :] = jnp.broadcast_to(pr, (SUBLANES, nst))
                pi_ref[kb, i * SUBLANES:(i + 1) * SUBLANES, :] = jnp.broadcast_to(pi, (SUBLANES, nst))
        car_r_ref[...] = jnp.zeros_like(car_r_ref)
        car_i_ref[...] = jnp.zeros_like(car_i_ref)

    nlb = D_MODEL // LANES
    for i in range(steps):
        for b in range(nlb):
            xp_ref[i * SUBLANES:(i + 1) * SUBLANES, b * LANES:(b + 1) * LANES] = (
                x_ref[pl.ds(i * nlb + b, SUBLANES, stride=steps * nlb), :])

    u = _rms(xp_ref[...], nmw_ref[...])
    ub = u.astype(BF16)

    half = nst // 2
    for kb in range(SSM_NBLK):
        bu_ref = bu_all_ref.at[kb % bu_all_ref.shape[0]]
        bu_ref[...] = jnp.dot(ub[:, kb * SSM_BLK_CH:(kb + 1) * SSM_BLK_CH], bm_ref[kb],
                              preferred_element_type=F32)
        for hh in range(2):
            re = slice(hh * half, (hh + 1) * half)
            im = slice(nst + hh * half, nst + (hh + 1) * half)
            ar = jnp.broadcast_to(ar_ref[kb, :, re], (SUBLANES, half))
            ai = jnp.broadcast_to(ai_ref[kb, :, re], (SUBLANES, half))

            def scan_step(i, s, re=re, im=im, ar=ar, ai=ai, bu_ref=bu_ref):
                sr, si = s
                r0 = pl.multiple_of(i * SUBLANES, SUBLANES)
                tr, ti = _cmul(ar, ai, sr, si)
                sr = tr + bu_ref[pl.ds(r0, SUBLANES), re]
                si = ti + bu_ref[pl.ds(r0, SUBLANES), im]
                bu_ref[pl.ds(r0, SUBLANES), re] = sr
                bu_ref[pl.ds(r0, SUBLANES), im] = si
                return sr, si

            zero = jnp.zeros((SUBLANES, half), F32)
            lax.fori_loop(0, steps, scan_step, (zero, zero), unroll=True)

        if chain:
            alr, ali = pr_ref[kb, rows - 1:rows, :], pi_ref[kb, rows - 1:rows, :]
            cr, ci = car_r_ref[kb], car_i_ref[kb]
            for j in range(SUBLANES):
                cr_ref[j:j + 1, :] = cr
                ci_ref[j:j + 1, :] = ci
                last = rows - SUBLANES + j
                tr, ti = _cmul(alr, ali, cr, ci)
                cr = tr + bu_ref[last:last + 1, 0:nst]
                ci = ti + bu_ref[last:last + 1, nst:2 * nst]
            car_r_ref[kb] = cr
            car_i_ref[kb] = ci
        else:
            cr_ref[...] = x0r_ref[:, kb * nst:(kb + 1) * nst]
            ci_ref[...] = x0i_ref[:, kb * nst:(kb + 1) * nst]

        c_r, c_i = cr_ref[...], ci_ref[...]

        def fix_step(i, _, kb=kb, c_r=c_r, c_i=c_i, bu_ref=bu_ref):
            r0 = pl.multiple_of(i * SUBLANES, SUBLANES)
            tr, ti = _cmul(pr_ref[kb, pl.ds(r0, SUBLANES), :], pi_ref[kb, pl.ds(r0, SUBLANES), :],
                           c_r, c_i)
            bu_ref[pl.ds(r0, SUBLANES), 0:nst] = bu_ref[pl.ds(r0, SUBLANES), 0:nst] + tr
            bu_ref[pl.ds(r0, SUBLANES), nst:2 * nst] = bu_ref[pl.ds(r0, SUBLANES), nst:2 * nst] + ti
            return 0

        lax.fori_loop(0, steps, fix_step, 0, unroll=True)
        sr_ref[:, kb * nst:(kb + 1) * nst] = bu_ref[rows - SUBLANES:rows, 0:nst]
        si_ref[:, kb * nst:(kb + 1) * nst] = bu_ref[rows - SUBLANES:rows, nst:2 * nst]
        y_ref[:, kb * SSM_BLK_CH:(kb + 1) * SSM_BLK_CH] = jnp.dot(
            bu_ref[...].astype(BF16), cm_ref[kb], preferred_element_type=F32)

    y = y_ref[...] + d_ref[...] * u
    z = jax.nn.gelu(y).astype(BF16)
    out = (jnp.dot(z, wglu_ref[...], preferred_element_type=F32)
           * jax.nn.sigmoid(jnp.dot(z, wgate_ref[...], preferred_element_type=F32)))
    y_ref[...] = xp_ref[...] + out
    for i in range(steps):
        for b in range(nlb):
            x2_ref[pl.ds(i * nlb + b, SUBLANES, stride=steps * nlb), :] = (
                y_ref[i * SUBLANES:(i + 1) * SUBLANES, b * LANES:(b + 1) * LANES])


def _s5(x, x0r, x0i, nmw, a_re, a_im, bm, cm, d, wglu, wgate, chain):
    nlb = D_MODEL // LANES
    n = x.shape[0] // nlb
    nchunk = n // SSM_ROWS
    row = lambda i: (i, 0)
    fix2 = lambda i: (0, 0)
    fix3 = lambda i: (0, 0, 0)
    state_map = fix2 if chain else row
    n_state_rows = SUBLANES if chain else SUBLANES * nchunk
    kern = functools.partial(_s5_kernel, chain=chain)
    return pl.pallas_call(
        kern,
        grid=(nchunk,),
        in_specs=[pl.BlockSpec((SSM_ROWS * nlb, LANES), row),
                  pl.BlockSpec((SUBLANES, N_STATES), state_map),
                  pl.BlockSpec((SUBLANES, N_STATES), state_map),
                  pl.BlockSpec((1, D_MODEL), fix2),
                  pl.BlockSpec((SSM_NBLK, 1, SSM_BLK_ST), fix3),
                  pl.BlockSpec((SSM_NBLK, 1, SSM_BLK_ST), fix3),
                  pl.BlockSpec((SSM_NBLK, SSM_BLK_CH, 2 * SSM_BLK_ST), fix3),
                  pl.BlockSpec((SSM_NBLK, 2 * SSM_BLK_ST, SSM_BLK_CH), fix3),
                  pl.BlockSpec((1, D_MODEL), fix2),
                  pl.BlockSpec((D_MODEL, D_MODEL), fix2),
                  pl.BlockSpec((D_MODEL, D_MODEL), fix2)],
        out_specs=[pl.BlockSpec((SSM_ROWS * nlb, LANES), row),
                   pl.BlockSpec((SUBLANES, N_STATES), state_map),
                   pl.BlockSpec((SUBLANES, N_STATES), state_map)],
        out_shape=[jax.ShapeDtypeStruct((n * nlb, LANES), F32),
                   jax.ShapeDtypeStruct((n_state_rows, N_STATES), F32),
                   jax.ShapeDtypeStruct((n_state_rows, N_STATES), F32)],
        scratch_shapes=[pltpu.VMEM((SSM_ROWS, D_MODEL), F32),
                        pltpu.VMEM((1, SSM_ROWS, 2 * SSM_BLK_ST), F32),
                        pltpu.VMEM((SSM_ROWS, D_MODEL), F32),
                        pltpu.VMEM((SSM_NBLK, SSM_ROWS, SSM_BLK_ST), F32),
                        pltpu.VMEM((SSM_NBLK, SSM_ROWS, SSM_BLK_ST), F32),
                        pltpu.VMEM((SUBLANES, SSM_BLK_ST), F32),
                        pltpu.VMEM((SUBLANES, SSM_BLK_ST), F32),
                        pltpu.VMEM((SSM_NBLK, 1, SSM_BLK_ST), F32),
                        pltpu.VMEM((SSM_NBLK, 1, SSM_BLK_ST), F32)],
        compiler_params=_cparams("arbitrary"),
        name="s5_mixer",
    )(x, x0r, x0i, nmw, a_re, a_im, bm, cm, d, wglu, wgate)


def _rope_cos_sin(pos):
    half = ROT_DIM // 2
    inv_freq = ROPE_THETA ** (-jnp.arange(half, dtype=F32) * 2.0 / ROT_DIM)
    ang = pos.astype(F32)[:, None] * inv_freq[None, :]
    return jnp.cos(ang), jnp.sin(ang)


def _rope_expand_matrix():
    half = ROT_DIM // 2
    e = np.zeros((ROT_DIM, 3 * LANES), np.float32)
    for lane in range(LANES):
        d = lane % HEAD_DIM
        if d < ROT_DIM:
            e[d % half, lane] = 1.0
        if d < half:
            e[half + d, LANES + lane] = -1.0
        elif d < ROT_DIM:
            e[half + d - half, 2 * LANES + lane] = 1.0
    return jnp.asarray(e)


def _rope_tables(pos):
    half = ROT_DIM // 2
    cos, sin = _rope_cos_sin(pos)
    n = pos.shape[0]
    pad = HEAD_DIM - ROT_DIM
    cosf = jnp.concatenate([cos, cos, jnp.ones((n, pad), F32)], axis=1)
    sina = jnp.concatenate([-sin, jnp.zeros((n, half + pad), F32)], axis=1)
    sinb = jnp.concatenate([jnp.zeros((n, half), F32), sin, jnp.zeros((n, pad), F32)], axis=1)
    tile2 = lambda t: jnp.concatenate([t, t], axis=1)
    return tile2(cosf), tile2(sina), tile2(sinb)


def _s5_params(lam_re, lam_im, log_dt, b_re, b_im, c_re, c_im):
    dt = jnp.exp(log_dt)[:, None]
    z_re, z_im = lam_re * dt, lam_im * dt
    mag = jnp.exp(z_re)
    lb_re, lb_im = mag * jnp.cos(z_im), mag * jnp.sin(z_im)
    n_re, n_im = lb_re - 1.0, lb_im
    den = lam_re * lam_re + lam_im * lam_im
    k_re = (n_re * lam_re + n_im * lam_im) / den
    k_im = (n_im * lam_re - n_re * lam_im) / den
    bb_re = k_re[..., None] * b_re - k_im[..., None] * b_im
    bb_im = k_re[..., None] * b_im + k_im[..., None] * b_re
    gpb = SSM_BLK_CH // SSM_GROUP
    eye = jnp.eye(gpb, dtype=F32)

    def in_blocks(bb):
        v = bb.reshape(SSM_NBLK, gpb, SSM_STATE, SSM_GROUP).transpose(0, 1, 3, 2)
        return jnp.einsum('kgcp,gh->kgchp', v, eye).reshape(SSM_NBLK, SSM_BLK_CH, SSM_BLK_ST)

    def out_blocks(cc):
        v = cc.reshape(SSM_NBLK, gpb, SSM_GROUP, SSM_STATE).transpose(0, 1, 3, 2)
        return jnp.einsum('kgpc,gh->kgphc', v, eye).reshape(SSM_NBLK, SSM_BLK_ST, SSM_BLK_CH)

    bm = jnp.concatenate([in_blocks(bb_re), in_blocks(bb_im)], axis=2).astype(BF16)
    cm = jnp.concatenate([out_blocks(c_re), -out_blocks(c_im)], axis=1).astype(BF16)
    a_re = lb_re.reshape(SSM_NBLK, 1, SSM_BLK_ST)
    a_im = lb_im.reshape(SSM_NBLK, 1, SSM_BLK_ST)
    return a_re, a_im, bm, cm


def _router_params(w_rg, b_rg, w_re, b_re):
    pad = LANES - N_EXPERT_GROUPS - N_EXPERTS
    wr = jnp.concatenate([w_rg, w_re, jnp.zeros((D_MODEL, pad), F32)], axis=1)
    w_hi = wr.astype(BF16)
    wr = jnp.concatenate([w_hi, (wr - w_hi.astype(F32)).astype(BF16)], axis=1)
    br = jnp.concatenate([b_rg, b_re, jnp.zeros((pad,), F32)])[None, :]
    return wr, br


def kernel(x_prompt, x_sample, cache_k, cache_v, state_ssm_re, state_ssm_im, norm_mix, norm_ffn, norm_final, attn_w_qkv, attn_lambda_q1, attn_lambda_k1, attn_lambda_q2, attn_lambda_k2, attn_subln, attn_w_o, ssm_lambda_re, ssm_lambda_im, ssm_log_dt, ssm_b_re, ssm_b_im, ssm_c_re, ssm_c_im, ssm_d, ssm_w_glu, ssm_w_gate, moe_w_router_group, moe_b_router_group, moe_w_router_expert, moe_b_router_expert, moe_w_gate, moe_w_up, moe_w_down):
    _, seq, _ = x_prompt.shape
    nb, dec, _ = x_sample.shape
    past = cache_k.shape[2]
    xp = x_prompt.reshape(seq, D_MODEL)
    xs = x_sample.reshape(nb * dec, D_MODEL)

    lam_init = 0.8 - 0.6 * math.exp(-0.3 * 0)
    lam = (jnp.exp(jnp.sum(attn_lambda_q1[0] * attn_lambda_k1[0]))
           - jnp.exp(jnp.sum(attn_lambda_q2[0] * attn_lambda_k2[0])) + lam_init)
    lam_p = jnp.full((1, ATT_T), lam, F32)
    lam_s = jnp.full((1, V_DIM), lam, F32)
    w_subln = attn_subln[0][None, :]
    out_scale = 1.0 - lam_init
    wqkv = attn_w_qkv[0].astype(BF16)
    wq, wk, wv = (wqkv[:, j * D_MODEL:(j + 1) * D_MODEL] for j in range(3))
    wo = attn_w_o[0].astype(BF16)
    nm0 = norm_mix[0][None, :]
    cos_p, sin_p = _rope_cos_sin(jnp.arange(seq, dtype=jnp.int32))
    tabs_s = tuple(jnp.tile(t, (nb, 1))
                   for t in _rope_tables(past + jnp.arange(dec, dtype=jnp.int32)))

    k_p, v_p, kb_p, qt_p, vt_p = _qkv_prompt(
        xp, nm0, wq.T, wk, wv, jnp.concatenate([cos_p, sin_p], axis=1),
        _rope_expand_matrix(), cos_p.T, sin_p.T, tm=512)
    q_s, k_s, v_s, kb_s, vb_s = _qkv(xs, nm0, wqkv, *tabs_s, tm=512)
    o_p = _attn_prompt(qt_p, kb_p, vt_p, w_subln, lam_p, out_scale)
    o_s = _attn_sample(q_s, kb_s, vb_s, cache_k[0].reshape(nb, past * N_HEADS, V_DIM),
                       cache_v[0].reshape(nb, past * N_HEADS, V_DIM), w_subln, lam_s, out_scale,
                       t=dec)

    nf0 = norm_ffn[0][None, :]
    wr0, br0 = _router_params(moe_w_router_group[0], moe_b_router_group[0],
                              moe_w_router_expert[0], moe_b_router_expert[0])
    wg0, wu0, wd0 = (moe_w_gate[0].astype(BF16), moe_w_up[0].astype(BF16),
                     moe_w_down[0].astype(BF16))
    nfin = norm_final[None, :]
    xp1 = _moe(xp, nf0, wr0, br0, wg0, wu0, wd0, nfin, False, tm=1024, out_rows=True,
               attn_proj=(o_p, wo))
    xs1 = _moe(xs, nf0, wr0, br0, wg0, wu0, wd0, nfin, False, tm=512, out_rows=True,
               attn_proj=(o_s, wo))

    a_re, a_im, bm, cm = _s5_params(ssm_lambda_re[0], ssm_lambda_im[0], ssm_log_dt[0],
                                    ssm_b_re[0], ssm_b_im[0], ssm_c_re[0], ssm_c_im[0])
    nm1 = norm_mix[1][None, :]
    nf1 = norm_ffn[1][None, :]
    wr1, br1 = _router_params(moe_w_router_group[1], moe_b_router_group[1],
                              moe_w_router_expert[1], moe_b_router_expert[1])
    s5w = (nm1, a_re, a_im, bm, cm, ssm_d[0][None, :], ssm_w_glu[0].astype(BF16),
           ssm_w_gate[0].astype(BF16))
    zero_state = jnp.zeros((SUBLANES, N_STATES), F32)
    x2_p, sr_p, si_p = _s5(xp1, zero_state, zero_state, *s5w, chain=True)
    x0r = state_ssm_re[0].reshape(nb, N_STATES)
    x0i = state_ssm_im[0].reshape(nb, N_STATES)
    x2_s, sr_s, si_s = _s5(xs1, x0r, x0i, *s5w, chain=False)

    wg1, wu1, wd1 = (moe_w_gate[1].astype(BF16), moe_w_up[1].astype(BF16),
                     moe_w_down[1].astype(BF16))
    y_p = _moe(x2_p, nf1, wr1, br1, wg1, wu1, wd1, nfin, True, tm=1024, in_rows=True)
    y_s = _moe(x2_s, nf1, wr1, br1, wg1, wu1, wd1, nfin, True, tm=512, in_rows=True)

    hshape = (N_HEADS, V_DIM)
    gshape = (N_SSM_GROUPS, SSM_STATE)
    return (y_p.reshape(1, seq, D_MODEL), y_s.reshape(nb, dec, D_MODEL),
            k_p.reshape(1, 1, seq, *hshape), v_p.reshape(1, 1, seq, *hshape),
            sr_p[SUBLANES - 1].reshape(1, 1, *gshape), si_p[SUBLANES - 1].reshape(1, 1, *gshape),
            k_s.reshape(1, nb, dec, *hshape), v_s.reshape(1, nb, dec, *hshape),
            sr_s.reshape(1, nb, *gshape), si_s.reshape(1, nb, *gshape))
```

```python
import functools
import math

import jax
import jax.numpy as jnp
import numpy as np
from jax import lax
from jax.experimental import pallas as pl
from jax.experimental.pallas import tpu as pltpu

F32 = jnp.float32
BF16 = jnp.bfloat16

D_MODEL = 1024
CHUNK = 64
N_HEADS = 8
HEAD_DIM = 64
V_DIM = 2 * HEAD_DIM
ROT_DIM = HEAD_DIM // 4
ROPE_THETA = 500000.0
SSM_GROUP = 16
N_SSM_GROUPS = D_MODEL // SSM_GROUP
SSM_STATE = 64
N_STATES = N_SSM_GROUPS * SSM_STATE
N_EXPERT_GROUPS = 4
EXPERTS_PER_GROUP = 4
N_EXPERTS = N_EXPERT_GROUPS * EXPERTS_PER_GROUP
D_EXPERT = D_MODEL // 4
RMS_EPS = 1e-6

LANES = 128
SUBLANES = 8
VMEM_LIMIT = 56 * 1024 * 1024

SSM_BLK_CH = 256
SSM_NBLK = D_MODEL // SSM_BLK_CH
SSM_BLK_ST = (SSM_BLK_CH // SSM_GROUP) * SSM_STATE
SSM_STEPS_PROMPT = 32


def _cparams(*sem, flags=None):
    return pltpu.CompilerParams(dimension_semantics=sem, vmem_limit_bytes=VMEM_LIMIT, flags=flags)


def _rms(x, w):
    return x * lax.rsqrt(jnp.mean(x * x, axis=-1, keepdims=True) + RMS_EPS) * w


def _qkv_kernel(x_ref, nw_ref, w_ref, cos_ref, sina_ref, sinb_ref,
                q_ref, k_ref, v_ref, kb_ref, vb_ref):
    h = _rms(x_ref[...], nw_ref[...]).astype(BF16)
    cosf, sina, sinb = cos_ref[...], sina_ref[...], sinb_ref[...]

    def rope(t):
        outs = []
        for hh in range(N_HEADS):
            xs = t[:, hh * LANES:(hh + 1) * LANES]
            outs.append(xs * cosf + pltpu.roll(xs, LANES - ROT_DIM // 2, 1) * sina
                        + pltpu.roll(xs, ROT_DIM // 2, 1) * sinb)
        return jnp.concatenate(outs, axis=1)

    q = rope(jnp.dot(h, w_ref[:, 0:D_MODEL], preferred_element_type=F32))
    q_ref[...] = (q * (HEAD_DIM ** -0.5)).astype(BF16)
    k = rope(jnp.dot(h, w_ref[:, D_MODEL:2 * D_MODEL], preferred_element_type=F32))
    k_ref[...] = k
    kb_ref[...] = k.astype(BF16)
    v = jnp.dot(h, w_ref[:, 2 * D_MODEL:3 * D_MODEL], preferred_element_type=F32)
    v_ref[...] = v
    vb_ref[...] = v.astype(BF16)


def _qkv(x, nw, w, cosf, sina, sinb, tm):
    n = x.shape[0]
    row = lambda i: (i, 0)
    fix = lambda i: (0, 0)
    return pl.pallas_call(
        _qkv_kernel,
        grid=(n // tm,),
        in_specs=[pl.BlockSpec((tm, D_MODEL), row), pl.BlockSpec((1, D_MODEL), fix),
                  pl.BlockSpec((D_MODEL, 3 * D_MODEL), fix),
                  pl.BlockSpec((tm, LANES), row), pl.BlockSpec((tm, LANES), row),
                  pl.BlockSpec((tm, LANES), row)],
        out_specs=[pl.BlockSpec((tm, D_MODEL), row)] * 5,
        out_shape=[jax.ShapeDtypeStruct((n, D_MODEL), BF16),
                   jax.ShapeDtypeStruct((n, D_MODEL), F32),
                   jax.ShapeDtypeStruct((n, D_MODEL), F32),
                   jax.ShapeDtypeStruct((n, D_MODEL), BF16),
                   jax.ShapeDtypeStruct((n, D_MODEL), BF16)],
        compiler_params=_cparams("arbitrary"),
        name="qkv_rope",
    )(x, nw, w, cosf, sina, sinb)


ATT_T = 512
ATT_K = 256
ATT_ONES = 16
ATT_UNROLL = 4
ATT_HEADS = 2
ATT_STRIP = 256


def _store_token_head_rows(ref, x):
    for r in range(x.shape[0] // SUBLANES):
        for hh in range(N_HEADS):
            ref[pl.ds(r * SUBLANES * N_HEADS + hh, SUBLANES, stride=N_HEADS), :] = (
                x[r * SUBLANES:(r + 1) * SUBLANES, hh * LANES:(hh + 1) * LANES])


def _qkv_prompt_kernel(x_ref, nw_ref, wqt_ref, wk_ref, wv_ref,
                       cs_ref, expand_ref, cost_ref, sint_ref,
                       k_ref, v_ref, kb_ref, qt_ref, vt_ref):
    h = _rms(x_ref[...], nw_ref[...]).astype(BF16)
    tabs = jnp.dot(cs_ref[...], expand_ref[...], precision=lax.Precision.HIGHEST,
                   preferred_element_type=F32)
    lane = lax.broadcasted_iota(jnp.int32, (1, LANES), 1)
    cosf = tabs[:, 0:LANES] + jnp.where(lane % HEAD_DIM >= ROT_DIM, 1.0, 0.0)
    sina, sinb = tabs[:, LANES:2 * LANES], tabs[:, 2 * LANES:3 * LANES]
    outs = []
    k = jnp.dot(h, wk_ref[...], preferred_element_type=F32)
    for hh in range(N_HEADS):
        xs = k[:, hh * LANES:(hh + 1) * LANES]
        outs.append(xs * cosf + pltpu.roll(xs, LANES - ROT_DIM // 2, 1) * sina
                    + pltpu.roll(xs, ROT_DIM // 2, 1) * sinb)
    k = jnp.concatenate(outs, axis=1)
    _store_token_head_rows(k_ref, k)
    kb_ref[...] = k.astype(BF16)
    v = jnp.dot(h, wv_ref[...], preferred_element_type=F32)
    _store_token_head_rows(v_ref, v)
    vt = v.T.astype(BF16)

    nt = (((1,), (1,)), ((), ()))
    qt = lax.dot_general(wqt_ref[...], h, nt, preferred_element_type=F32) * (
        HEAD_DIM ** -0.5 * math.log2(math.e))
    cost, sint = cost_ref[...], sint_ref[...]
    half = ROT_DIM // 2
    pieces = []
    for blk in range(D_MODEL // HEAD_DIM):
        r0 = blk * HEAD_DIM
        x1, x2 = qt[r0:r0 + half], qt[r0 + half:r0 + ROT_DIM]
        pieces += [x1 * cost - x2 * sint, x2 * cost + x1 * sint, qt[r0 + ROT_DIM:r0 + HEAD_DIM]]
    qt = jnp.concatenate(pieces, axis=0).astype(BF16)
    for b in range(qt_ref.shape[0]):
        qt_ref[b] = qt[:, b * ATT_T:(b + 1) * ATT_T]
    for b in range(vt_ref.shape[0]):
        vt_ref[b] = vt[:, b * ATT_K:(b + 1) * ATT_K]


def _qkv_prompt(x, nw, wqt, wk, wv, cs, expand, cost, sint, tm):
    n = x.shape[0]
    row = lambda i: (i, 0)
    fix = lambda i: (0, 0)
    wspec = pl.BlockSpec((D_MODEL, D_MODEL), fix)
    ttspec = pl.BlockSpec((ROT_DIM // 2, tm), lambda i: (0, i))
    thspec = pl.BlockSpec((tm * N_HEADS, V_DIM), row)
    qblk = pl.BlockSpec((tm // ATT_T, D_MODEL, ATT_T), lambda i: (i, 0, 0))
    vblk = pl.BlockSpec((tm // ATT_K, D_MODEL, ATT_K), lambda i: (i, 0, 0))
    return pl.pallas_call(
        _qkv_prompt_kernel,
        grid=(n // tm,),
        in_specs=[pl.BlockSpec((tm, D_MODEL), row), pl.BlockSpec((1, D_MODEL), fix),
                  wspec, wspec, wspec,
                  pl.BlockSpec((tm, ROT_DIM), row), pl.BlockSpec((ROT_DIM, 3 * LANES), fix),
                  ttspec, ttspec],
        out_specs=[thspec, thspec, pl.BlockSpec((tm, D_MODEL), row), qblk, vblk],
        out_shape=[jax.ShapeDtypeStruct((n * N_HEADS, V_DIM), F32),
                   jax.ShapeDtypeStruct((n * N_HEADS, V_DIM), F32),
                   jax.ShapeDtypeStruct((n, D_MODEL), BF16),
                   jax.ShapeDtypeStruct((n // ATT_T, D_MODEL, ATT_T), BF16),
                   jax.ShapeDtypeStruct((n // ATT_K, D_MODEL, ATT_K), BF16)],
        compiler_params=_cparams("arbitrary"),
        name="qkv_rope_prompt",
    )(x, nw, wqt, wk, wv, cs, expand, cost, sint)


def _stack_subheads(q):
    lane = lax.broadcasted_iota(jnp.int32, q.shape, 1)
    zero = jnp.zeros_like(q)
    return jnp.concatenate([jnp.where(lane < HEAD_DIM, q, zero),
                            jnp.where(lane >= HEAD_DIM, q, zero)], axis=0)


def _scores(qs, k):
    return lax.dot_general(qs, k, (((1,), (1,)), ((), ())), preferred_element_type=F32)


def _diff_finish(acc, l, t, lam, w, out_scale):
    o = acc[:t] / l[:t] - lam * (acc[t:] / l[t:])
    return _rms(o, w) * out_scale


def _attn_prompt_kernel(qt_ref, k_ref, vt_ref, w_ref, lam_ref, o_ref, sa_ref, sb_ref, acc_ref,
                        bias_ref, *, out_scale):
    t, tk = ATT_T, ATT_K
    heads = range(ATT_HEADS)
    i = pl.program_id(1)
    row = lax.broadcasted_iota(jnp.int32, (V_DIM, t), 0)
    qs = []
    for g in heads:
        qt = qt_ref[0, g * V_DIM:(g + 1) * V_DIM, :]
        zero = jnp.zeros_like(qt)
        qs.append(jnp.concatenate([jnp.where(row < HEAD_DIM, qt, zero),
                                   jnp.where(row >= HEAD_DIM, qt, zero)], axis=1))

    all_strips = tuple(range(2 * t // ATT_STRIP))
    late_strips = tuple(c for c in all_strips if (c * ATT_STRIP) % t + ATT_STRIP > t - tk)

    def scores(ref, g, j, bias=None, strips=all_strips):
        r0 = pl.multiple_of(j * tk, tk)
        kblk = k_ref[pl.ds(r0, tk), g * V_DIM:(g + 1) * V_DIM]
        smax = []
        for c in all_strips:
            if c not in strips:
                smax.append(jnp.full((1, ATT_STRIP), -jnp.inf, F32))
                continue
            cols = slice(c * ATT_STRIP, (c + 1) * ATT_STRIP)
            s = jnp.dot(kblk, qs[g][:, cols], preferred_element_type=F32)
            if bias is not None:
                s = s + bias[:, cols]
            ref[g, :, cols] = s
            smax.append(jnp.max(s, axis=0, keepdims=True))
        return jnp.concatenate(smax, axis=1)

    ones = jnp.ones((ATT_ONES, tk), BF16)

    def update(g, m, ref, smax, j, strips=all_strips):
        m_new = jnp.maximum(m, smax)
        alpha = jnp.exp2(m - m_new)
        v1 = jnp.concatenate([vt_ref[j, g * V_DIM:(g + 1) * V_DIM, :], ones], axis=0)
        for c in strips:
            cols = slice(c * ATT_STRIP, (c + 1) * ATT_STRIP)
            p = jnp.exp2(ref[g, :, cols] - m_new[:, cols]).astype(BF16)
            acc_ref[g, :, cols] = (alpha[:, cols] * acc_ref[g, :, cols]
                                   + jnp.dot(v1, p, preferred_element_type=F32))
        return m_new

    @pl.when((pl.program_id(0) == 0) & (i == 0))
    def _():
        for b in range(t // tk):
            kc = (b * tk + lax.broadcasted_iota(jnp.int32, (tk, 2 * t), 0)) // CHUNK
            qc = (lax.broadcasted_iota(jnp.int32, (tk, 2 * t), 1) % t) // CHUNK
            bias_ref[b] = jnp.where(kc <= qc, 0.0, -jnp.inf)

    acc_ref[...] = jnp.zeros(acc_ref.shape, F32)
    first_bias = jnp.where(i == 0, bias_ref[0], 0.0)
    xa = tuple(scores(sa_ref, g, 0, first_bias) for g in heads)

    def pairs(n, last_bias=None):
        def body(jj, carry):
            ms, xa = list(carry[0]), list(carry[1])
            for u in range(n):
                j = 2 * (jj * n + u)
                xb = [scores(sb_ref, g, j + 1) for g in heads]
                for g in heads:
                    ms[g] = update(g, ms[g], sa_ref, xa[g], j)
                for g in heads:
                    xa[g] = scores(sa_ref, g, j + 2, last_bias)
                for g in heads:
                    ms[g] = update(g, ms[g], sb_ref, xb[g], j + 1)
            return tuple(ms), tuple(xa)
        return body

    ms = tuple(jnp.full((1, 2 * t), -jnp.inf, F32) for _ in heads)
    nfree = jnp.maximum(i - 1, 0)
    nlong = nfree // ATT_UNROLL
    carry = lax.fori_loop(0, nlong, pairs(ATT_UNROLL), (ms, xa))
    carry = lax.fori_loop(nlong * ATT_UNROLL, nfree, pairs(1), carry)
    ms, xa = lax.fori_loop(nfree, i, pairs(1, bias_ref[0]), carry)

    xb = [scores(sb_ref, g, 2 * i + 1, bias_ref[1], late_strips) for g in heads]
    ms = [update(g, ms[g], sa_ref, xa[g], 2 * i) for g in heads]
    for g in heads:
        update(g, ms[g], sb_ref, xb[g], 2 * i + 1, late_strips)

    for g in heads:
        acc = acc_ref[g, 0:V_DIM, :]
        l = acc_ref[g, V_DIM:V_DIM + 1, :]
        o = acc[:, :t] / l[:, :t] - lam_ref[...] * (acc[:, t:] / l[:, t:])
        o = o * lax.rsqrt(jnp.mean(o * o, axis=0, keepdims=True) + RMS_EPS)
        o_ref[:, g * V_DIM:(g + 1) * V_DIM] = (o.T * w_ref[...] * out_scale).astype(BF16)


def _attn_prompt(qt, kb, vt, w_subln, lam, out_scale):
    l = kb.shape[0]
    t, tk, ng = ATT_T, ATT_K, ATT_HEADS
    kern = functools.partial(_attn_prompt_kernel, out_scale=out_scale)
    return pl.pallas_call(
        kern,
        grid=(N_HEADS // ng, l // t),
        in_specs=[pl.BlockSpec((1, ng * V_DIM, t), lambda h, i: (i, h, 0)),
                  pl.BlockSpec((l, ng * V_DIM), lambda h, i: (0, h)),
                  pl.BlockSpec((l // tk, ng * V_DIM, tk), lambda h, i: (0, h, 0)),
                  pl.BlockSpec((1, V_DIM), lambda h, i: (0, 0)),
                  pl.BlockSpec((1, t), lambda h, i: (0, 0))],
        out_specs=pl.BlockSpec((t, ng * V_DIM), lambda h, i: (i, h)),
        out_shape=jax.ShapeDtypeStruct((l, D_MODEL), BF16),
        scratch_shapes=[pltpu.VMEM((ng, tk, 2 * t), F32), pltpu.VMEM((ng, tk, 2 * t), F32),
                        pltpu.VMEM((ng, V_DIM + ATT_ONES, 2 * t), F32),
                        pltpu.VMEM((t // tk, tk, 2 * t), F32)],
        compiler_params=_cparams("arbitrary", "arbitrary"),
        name="attn_prompt",
    )(qt, kb, vt, w_subln, lam)


def _attn_sample_kernel(q_ref, kn_ref, vn_ref, ck_ref, cv_ref, w_ref, lam_ref, o_ref,
                        *, t, out_scale):
    for hh in range(N_HEADS):
        cols = slice(hh * V_DIM, (hh + 1) * V_DIM)
        qs = _stack_subheads(q_ref[:, cols])
        past = ck_ref.shape[1] // N_HEADS
        kc = ck_ref[0, pl.ds(hh, past, stride=N_HEADS), :].astype(BF16)
        vc = cv_ref[0, pl.ds(hh, past, stride=N_HEADS), :].astype(BF16)
        s_c = _scores(qs, kc)
        s_n = _scores(qs, kn_ref[:, cols])
        m = jnp.maximum(jnp.max(s_c, axis=-1, keepdims=True), jnp.max(s_n, axis=-1, keepdims=True))
        p_c = jnp.exp(s_c - m)
        p_n = jnp.exp(s_n - m)
        l = jnp.sum(p_c, axis=-1, keepdims=True) + jnp.sum(p_n, axis=-1, keepdims=True)
        acc = (jnp.dot(p_c.astype(BF16), vc, preferred_element_type=F32)
               + jnp.dot(p_n.astype(BF16), vn_ref[:, cols], preferred_element_type=F32))
        o_ref[:, cols] = _diff_finish(acc, l, t, lam_ref[...], w_ref[...], out_scale).astype(BF16)


def _attn_sample(q, kb, vb, cache_k, cache_v, w_subln, lam, out_scale, t):
    nb, rows, _ = cache_k.shape
    kern = functools.partial(_attn_sample_kernel, t=t, out_scale=out_scale)
    row = lambda b: (b, 0)
    return pl.pallas_call(
        kern,
        grid=(nb,),
        in_specs=[pl.BlockSpec((t, D_MODEL), row), pl.BlockSpec((t, D_MODEL), row),
                  pl.BlockSpec((t, D_MODEL), row),
                  pl.BlockSpec((1, rows, V_DIM), lambda b: (b, 0, 0)),
                  pl.BlockSpec((1, rows, V_DIM), lambda b: (b, 0, 0)),
                  pl.BlockSpec((1, V_DIM), lambda b: (0, 0)),
                  pl.BlockSpec((1, V_DIM), lambda b: (0, 0))],
        out_specs=pl.BlockSpec((t, D_MODEL), row),
        out_shape=jax.ShapeDtypeStruct((nb * t, D_MODEL), BF16),
        compiler_params=_cparams("arbitrary"),
        name="attn_sample",
    )(q, kb, vb, cache_k, cache_v, w_subln, lam)


def _route(x, nw, wr, br):
    hf = _rms(x, nw)
    h = hf.astype(BF16)
    h_lo = (hf - h.astype(F32)).astype(BF16)
    hw = jnp.dot(h, wr, preferred_element_type=F32)
    lg = (hw[:, :LANES] + hw[:, LANES:] + jnp.dot(h_lo, wr[:, :LANES], preferred_element_type=F32)
          + br)
    lane = lax.broadcasted_iota(jnp.int32, lg.shape, 1)
    big = jnp.int32(LANES)
    neg = jnp.float32(-jnp.inf)

    def masked_softmax(mask):
        z = jnp.where(mask, lg, neg)
        e = jnp.exp(z - jnp.max(z, axis=-1, keepdims=True))
        return e / jnp.sum(e, axis=-1, keepdims=True)

    def top1(p, mask):
        v = jnp.max(jnp.where(mask, p, -1.0), axis=-1, keepdims=True)
        idx = jnp.min(jnp.where(mask & (p == v), lane, big), axis=-1, keepdims=True)
        return v, idx

    gmask = lane < N_EXPERT_GROUPS
    g_val, g_idx = top1(masked_softmax(gmask), gmask)
    lo = N_EXPERT_GROUPS + g_idx * EXPERTS_PER_GROUP
    emask = (lane >= lo) & (lane < lo + EXPERTS_PER_GROUP)
    pe = masked_softmax(emask)
    v1, i1 = top1(pe, emask)
    v2, i2 = top1(pe, emask & (lane != i1))
    den = v1 + v2
    w1 = g_val * v1 / den
    w2 = g_val * v2 / den
    comb = (jnp.where(lane == i1 - N_EXPERT_GROUPS, w1, 0.0)
            + jnp.where(lane == i2 - N_EXPERT_GROUPS, w2, 0.0))
    return h, comb


def _moe_kernel(*refs, final_norm, in_rows, out_rows, attn_proj):
    if attn_proj:
        x_ref, o_ref, wo_ref = refs[:3]
        refs = refs[3:]
    else:
        x_ref, refs = refs[0], refs[1:]
    (nfw_ref, wr_ref, br_ref, wg_ref, wu_ref, wd_ref, nw_ref, y_ref,
     h_ref, c_ref, acc_ref) = refs
    g = pl.program_id(1)
    nlb = D_MODEL // LANES

    @pl.when(g == 0)
    def _():
        if in_rows:
            for r in range(acc_ref.shape[0] // SUBLANES):
                for b in range(nlb):
                    acc_ref[r * SUBLANES:(r + 1) * SUBLANES, b * LANES:(b + 1) * LANES] = (
                        x_ref[pl.ds(r * SUBLANES * nlb + b, SUBLANES, stride=nlb), :])
        elif attn_proj:
            acc_ref[...] = x_ref[...] + jnp.dot(o_ref[...], wo_ref[...],
                                                preferred_element_type=F32)
        else:
            acc_ref[...] = x_ref[...]
        h_ref[...], c_ref[...] = _route(acc_ref[...], nfw_ref[...], wr_ref[...], br_ref[...])

    h = h_ref[...]
    comb = c_ref[...]
    acc = acc_ref[...]
    for e in range(EXPERTS_PER_GROUP):
        gate = jnp.dot(h, wg_ref[e], preferred_element_type=F32)
        up = jnp.dot(h, wu_ref[e], preferred_element_type=F32)
        hid = (gate * jax.nn.sigmoid(gate) * up).astype(BF16)
        lane = lax.broadcasted_iota(jnp.int32, comb.shape, 1)
        ce = jnp.sum(jnp.where(lane == g * EXPERTS_PER_GROUP + e, comb, 0.0),
                     axis=-1, keepdims=True)
        acc = acc + ce * jnp.dot(hid, wd_ref[e], preferred_element_type=F32)
    acc_ref[...] = acc

    @pl.when(g == N_EXPERT_GROUPS - 1)
    def _():
        y = acc_ref[...]
        if final_norm:
            y = _rms(y, nw_ref[...])
        if out_rows:
            _store_token_head_rows(y_ref, y)
        else:
            y_ref[...] = y


def _moe(x, nfw, wr, br, wg, wu, wd, nw, final_norm, tm, in_rows=False, out_rows=False,
         attn_proj=None):
    nlb = D_MODEL // LANES
    n = x.shape[0] // nlb if in_rows else x.shape[0]
    row = lambda i, g: (i, 0)
    fix = lambda i, g: (0, 0)
    rows_spec = pl.BlockSpec((tm * nlb, LANES), row)
    nat_spec = pl.BlockSpec((tm, D_MODEL), row)
    kern = functools.partial(_moe_kernel, final_norm=final_norm, in_rows=in_rows,
                             out_rows=out_rows, attn_proj=attn_proj is not None)
    proj_specs = [nat_spec, pl.BlockSpec((D_MODEL, D_MODEL), fix)] if attn_proj else []
    return pl.pallas_call(
        kern,
        grid=(n // tm, N_EXPERT_GROUPS),
        in_specs=[rows_spec if in_rows else nat_spec] + proj_specs + [
                  pl.BlockSpec((1, D_MODEL), fix),
                  pl.BlockSpec((D_MODEL, 2 * LANES), fix), pl.BlockSpec((1, LANES), fix),
                  pl.BlockSpec((EXPERTS_PER_GROUP, D_MODEL, D_EXPERT), lambda i, g: (g, 0, 0)),
                  pl.BlockSpec((EXPERTS_PER_GROUP, D_MODEL, D_EXPERT), lambda i, g: (g, 0, 0)),
                  pl.BlockSpec((EXPERTS_PER_GROUP, D_EXPERT, D_MODEL), lambda i, g: (g, 0, 0)),
                  pl.BlockSpec((1, D_MODEL), lambda i, g: (0, 0))],
        out_specs=rows_spec if out_rows else nat_spec,
        out_shape=jax.ShapeDtypeStruct((n * nlb, LANES) if out_rows else (n, D_MODEL), F32),
        scratch_shapes=[pltpu.VMEM((tm, D_MODEL), BF16), pltpu.VMEM((tm, LANES), F32),
                        pltpu.VMEM((tm, D_MODEL), F32)],
        compiler_params=_cparams("arbitrary", "arbitrary"),
        name="moe",
    )(x, *(attn_proj or ()), nfw, wr, br, wg, wu, wd, nw)


def _cmul(ar, ai, br, bi):
    return ar * br - ai * bi, ar * bi + ai * br


def _s5_kernel(x_ref, x0r_ref, x0i_ref, nmw_ref, ar_ref, ai_ref, bm_ref, cm_ref, d_ref,
               wglu_ref, wgate_ref,
               x2_ref, sr_ref, si_ref,
               xp_ref, bu_all_ref, y_ref, pr_ref, pi_ref, cr_ref, ci_ref, car_r_ref, car_i_ref,
               *, chain, steps):
    rows, nst = SUBLANES * steps, SSM_BLK_ST

    @pl.when(pl.program_id(0) == 0)
    def _():
        for kb in range(SSM_NBLK):
            ar, ai = ar_ref[kb], ai_ref[kb]
            pr, pi = ar, ai
            for i in range(steps):
                if i:
                    pr, pi = _cmul(ar, ai, pr, pi)
                pr_ref[kb, i * SUBLANES:(i + 1) * SUBLANES, :] = jnp.broadcast_to(pr, (SUBLANES, nst))
                pi_ref[kb, i * SUBLANES:(i + 1) * SUBLANES, :] = jnp.broadcast_to(pi, (SUBLANES, nst))
        car_r_ref[...] = jnp.zeros_like(car_r_ref)
        car_i_ref[...] = jnp.zeros_like(car_i_ref)

    nlb = D_MODEL // LANES
    for i in range(steps):
        for b in range(nlb):
            xp_ref[i * SUBLANES:(i + 1) * SUBLANES, b * LANES:(b + 1) * LANES] = (
                x_ref[pl.ds(i * nlb + b, SUBLANES, stride=steps * nlb), :])

    u = _rms(xp_ref[...], nmw_ref[...])
    ub = u.astype(BF16)

    half = nst // 2
    for kb in range(SSM_NBLK):
        bu_ref = bu_all_ref.at[kb % bu_all_ref.shape[0]]
        bu_ref[...] = jnp.dot(ub[:, kb * SSM_BLK_CH:(kb + 1) * SSM_BLK_CH], bm_ref[kb],
                              preferred_element_type=F32)
        for hh in range(2):
            re = slice(hh * half, (hh + 1) * half)
            im = slice(nst + hh * half, nst + (hh + 1) * half)
            ar = jnp.broadcast_to(ar_ref[kb, :, re], (SUBLANES, half))
            ai = jnp.broadcast_to(ai_ref[kb, :, re], (SUBLANES, half))

            def scan_step(i, s, re=re, im=im, ar=ar, ai=ai, bu_ref=bu_ref):
                sr, si = s
                r0 = pl.multiple_of(i * SUBLANES, SUBLANES)
                tr, ti = _cmul(ar, ai, sr, si)
                sr = tr + bu_ref[pl.ds(r0, SUBLANES), re]
                si = ti + bu_ref[pl.ds(r0, SUBLANES), im]
                bu_ref[pl.ds(r0, SUBLANES), re] = sr
                bu_ref[pl.ds(r0, SUBLANES), im] = si
                return sr, si

            zero = jnp.zeros((SUBLANES, half), F32)
            lax.fori_loop(0, steps, scan_step, (zero, zero), unroll=True)

        if chain:
            alr, ali = pr_ref[kb, rows - 1:rows, :], pi_ref[kb, rows - 1:rows, :]
            cr, ci = car_r_ref[kb], car_i_ref[kb]
            for j in range(SUBLANES):
                cr_ref[j:j + 1, :] = cr
                ci_ref[j:j + 1, :] = ci
                last = rows - SUBLANES + j
                tr, ti = _cmul(alr, ali, cr, ci)
                cr = tr + bu_ref[last:last + 1, 0:nst]
                ci = ti + bu_ref[last:last + 1, nst:2 * nst]
            car_r_ref[kb] = cr
            car_i_ref[kb] = ci
        else:
            cr_ref[...] = x0r_ref[:, kb * nst:(kb + 1) * nst]
            ci_ref[...] = x0i_ref[:, kb * nst:(kb + 1) * nst]

        c_r, c_i = cr_ref[...], ci_ref[...]

        def fix_step(i, _, kb=kb, c_r=c_r, c_i=c_i, bu_ref=bu_ref):
            r0 = pl.multiple_of(i * SUBLANES, SUBLANES)
            tr, ti = _cmul(pr_ref[kb, pl.ds(r0, SUBLANES), :], pi_ref[kb, pl.ds(r0, SUBLANES), :],
                           c_r, c_i)
            bu_ref[pl.ds(r0, SUBLANES), 0:nst] = bu_ref[pl.ds(r0, SUBLANES), 0:nst] + tr
            bu_ref[pl.ds(r0, SUBLANES), nst:2 * nst] = bu_ref[pl.ds(r0, SUBLANES), nst:2 * nst] + ti
            return 0

        lax.fori_loop(0, steps, fix_step, 0, unroll=True)
        sr_ref[:, kb * nst:(kb + 1) * nst] = bu_ref[rows - SUBLANES:rows, 0:nst]
        si_ref[:, kb * nst:(kb + 1) * nst] = bu_ref[rows - SUBLANES:rows, nst:2 * nst]
        y_ref[:, kb * SSM_BLK_CH:(kb + 1) * SSM_BLK_CH] = jnp.dot(
            bu_ref[...].astype(BF16), cm_ref[kb], preferred_element_type=F32)

    y = y_ref[...] + d_ref[...] * u
    z = jax.nn.gelu(y).astype(BF16)
    out = (jnp.dot(z, wglu_ref[...], preferred_element_type=F32)
           * jax.nn.sigmoid(jnp.dot(z, wgate_ref[...], preferred_element_type=F32)))
    y_ref[...] = xp_ref[...] + out
    for i in range(steps):
        for b in range(nlb):
            x2_ref[pl.ds(i * nlb + b, SUBLANES, stride=steps * nlb), :] = (
                y_ref[i * SUBLANES:(i + 1) * SUBLANES, b * LANES:(b + 1) * LANES])


def _s5(x, x0r, x0i, nmw, a_re, a_im, bm, cm, d, wglu, wgate, chain, steps):
    nlb = D_MODEL // LANES
    n = x.shape[0] // nlb
    rows = SUBLANES * steps
    nchunk = n // rows
    row = lambda i: (i, 0)
    fix2 = lambda i: (0, 0)
    fix3 = lambda i: (0, 0, 0)
    state_map = fix2 if chain else row
    n_state_rows = SUBLANES if chain else SUBLANES * nchunk
    kern = functools.partial(_s5_kernel, chain=chain, steps=steps)
    return pl.pallas_call(
        kern,
        grid=(nchunk,),
        in_specs=[pl.BlockSpec((rows * nlb, LANES), row),
                  pl.BlockSpec((SUBLANES, N_STATES), state_map),
                  pl.BlockSpec((SUBLANES, N_STATES), state_map),
                  pl.BlockSpec((1, D_MODEL), fix2),
                  pl.BlockSpec((SSM_NBLK, 1, SSM_BLK_ST), fix3),
                  pl.BlockSpec((SSM_NBLK, 1, SSM_BLK_ST), fix3),
                  pl.BlockSpec((SSM_NBLK, SSM_BLK_CH, 2 * SSM_BLK_ST), fix3),
                  pl.BlockSpec((SSM_NBLK, 2 * SSM_BLK_ST, SSM_BLK_CH), fix3),
                  pl.BlockSpec((1, D_MODEL), fix2),
                  pl.BlockSpec((D_MODEL, D_MODEL), fix2),
                  pl.BlockSpec((D_MODEL, D_MODEL), fix2)],
        out_specs=[pl.BlockSpec((rows * nlb, LANES), row),
                   pl.BlockSpec((SUBLANES, N_STATES), state_map),
                   pl.BlockSpec((SUBLANES, N_STATES), state_map)],
        out_shape=[jax.ShapeDtypeStruct((n * nlb, LANES), F32),
                   jax.ShapeDtypeStruct((n_state_rows, N_STATES), F32),
                   jax.ShapeDtypeStruct((n_state_rows, N_STATES), F32)],
        scratch_shapes=[pltpu.VMEM((rows, D_MODEL), F32),
                        pltpu.VMEM((1, rows, 2 * SSM_BLK_ST), F32),
                        pltpu.VMEM((rows, D_MODEL), F32),
                        pltpu.VMEM((SSM_NBLK, rows, SSM_BLK_ST), F32),
                        pltpu.VMEM((SSM_NBLK, rows, SSM_BLK_ST), F32),
                        pltpu.VMEM((SUBLANES, SSM_BLK_ST), F32),
                        pltpu.VMEM((SUBLANES, SSM_BLK_ST), F32),
                        pltpu.VMEM((SSM_NBLK, 1, SSM_BLK_ST), F32),
                        pltpu.VMEM((SSM_NBLK, 1, SSM_BLK_ST), F32)],
        compiler_params=_cparams("arbitrary"),
        name="s5_mixer",
    )(x, x0r, x0i, nmw, a_re, a_im, bm, cm, d, wglu, wgate)


def _rope_cos_sin(pos):
    half = ROT_DIM // 2
    inv_freq = ROPE_THETA ** (-jnp.arange(half, dtype=F32) * 2.0 / ROT_DIM)
    ang = pos.astype(F32)[:, None] * inv_freq[None, :]
    return jnp.cos(ang), jnp.sin(ang)


def _rope_expand_matrix():
    half = ROT_DIM // 2
    e = np.zeros((ROT_DIM, 3 * LANES), np.float32)
    for lane in range(LANES):
        d = lane % HEAD_DIM
        if d < ROT_DIM:
            e[d % half, lane] = 1.0
        if d < half:
            e[half + d, LANES + lane] = -1.0
        elif d < ROT_DIM:
            e[half + d - half, 2 * LANES + lane] = 1.0
    return jnp.asarray(e)


def _rope_tables(pos):
    half = ROT_DIM // 2
    cos, sin = _rope_cos_sin(pos)
    n = pos.shape[0]
    pad = HEAD_DIM - ROT_DIM
    cosf = jnp.concatenate([cos, cos, jnp.ones((n, pad), F32)], axis=1)
    sina = jnp.concatenate([-sin, jnp.zeros((n, half + pad), F32)], axis=1)
    sinb = jnp.concatenate([jnp.zeros((n, half), F32), sin, jnp.zeros((n, pad), F32)], axis=1)
    tile2 = lambda t: jnp.concatenate([t, t], axis=1)
    return tile2(cosf), tile2(sina), tile2(sinb)


def _s5_params(lam_re, lam_im, log_dt, b_re, b_im, c_re, c_im):
    dt = jnp.exp(log_dt)[:, None]
    z_re, z_im = lam_re * dt, lam_im * dt
    mag = jnp.exp(z_re)
    lb_re, lb_im = mag * jnp.cos(z_im), mag * jnp.sin(z_im)
    n_re, n_im = lb_re - 1.0, lb_im
    den = lam_re * lam_re + lam_im * lam_im
    k_re = (n_re * lam_re + n_im * lam_im) / den
    k_im = (n_im * lam_re - n_re * lam_im) / den
    bb_re = k_re[..., None] * b_re - k_im[..., None] * b_im
    bb_im = k_re[..., None] * b_im + k_im[..., None] * b_re
    gpb = SSM_BLK_CH // SSM_GROUP
    eye = jnp.eye(gpb, dtype=F32)

    def in_blocks(bb):
        v = bb.reshape(SSM_NBLK, gpb, SSM_STATE, SSM_GROUP).transpose(0, 1, 3, 2)
        return jnp.einsum('kgcp,gh->kgchp', v, eye).reshape(SSM_NBLK, SSM_BLK_CH, SSM_BLK_ST)

    def out_blocks(cc):
        v = cc.reshape(SSM_NBLK, gpb, SSM_GROUP, SSM_STATE).transpose(0, 1, 3, 2)
        return jnp.einsum('kgpc,gh->kgphc', v, eye).reshape(SSM_NBLK, SSM_BLK_ST, SSM_BLK_CH)

    bm = jnp.concatenate([in_blocks(bb_re), in_blocks(bb_im)], axis=2).astype(BF16)
    cm = jnp.concatenate([out_blocks(c_re), -out_blocks(c_im)], axis=1).astype(BF16)
    a_re = lb_re.reshape(SSM_NBLK, 1, SSM_BLK_ST)
    a_im = lb_im.reshape(SSM_NBLK, 1, SSM_BLK_ST)
    return a_re, a_im, bm, cm


def _router_params(w_rg, b_rg, w_re, b_re):
    pad = LANES - N_EXPERT_GROUPS - N_EXPERTS
    wr = jnp.concatenate([w_rg, w_re, jnp.zeros((D_MODEL, pad), F32)], axis=1)
    w_hi = wr.astype(BF16)
    wr = jnp.concatenate([w_hi, (wr - w_hi.astype(F32)).astype(BF16)], axis=1)
    br = jnp.concatenate([b_rg, b_re, jnp.zeros((pad,), F32)])[None, :]
    return wr, br


def kernel(x_prompt, x_sample, cache_k, cache_v, state_ssm_re, state_ssm_im, norm_mix, norm_ffn, norm_final, attn_w_qkv, attn_lambda_q1, attn_lambda_k1, attn_lambda_q2, attn_lambda_k2, attn_subln, attn_w_o, ssm_lambda_re, ssm_lambda_im, ssm_log_dt, ssm_b_re, ssm_b_im, ssm_c_re, ssm_c_im, ssm_d, ssm_w_glu, ssm_w_gate, moe_w_router_group, moe_b_router_group, moe_w_router_expert, moe_b_router_expert, moe_w_gate, moe_w_up, moe_w_down):
    _, seq, _ = x_prompt.shape
    nb, dec, _ = x_sample.shape
    past = cache_k.shape[2]
    xp = x_prompt.reshape(seq, D_MODEL)
    xs = x_sample.reshape(nb * dec, D_MODEL)

    lam_init = 0.8 - 0.6 * math.exp(-0.3 * 0)
    lam = (jnp.exp(jnp.sum(attn_lambda_q1[0] * attn_lambda_k1[0]))
           - jnp.exp(jnp.sum(attn_lambda_q2[0] * attn_lambda_k2[0])) + lam_init)
    lam_p = jnp.full((1, ATT_T), lam, F32)
    lam_s = jnp.full((1, V_DIM), lam, F32)
    w_subln = attn_subln[0][None, :]
    out_scale = 1.0 - lam_init
    wqkv = attn_w_qkv[0].astype(BF16)
    wq, wk, wv = (wqkv[:, j * D_MODEL:(j + 1) * D_MODEL] for j in range(3))
    wo = attn_w_o[0].astype(BF16)
    nm0 = norm_mix[0][None, :]
    cos_p, sin_p = _rope_cos_sin(jnp.arange(seq, dtype=jnp.int32))
    tabs_s = tuple(jnp.tile(t, (nb, 1))
                   for t in _rope_tables(past + jnp.arange(dec, dtype=jnp.int32)))

    k_p, v_p, kb_p, qt_p, vt_p = _qkv_prompt(
        xp, nm0, wq.T, wk, wv, jnp.concatenate([cos_p, sin_p], axis=1),
        _rope_expand_matrix(), cos_p.T, sin_p.T, tm=512)
    q_s, k_s, v_s, kb_s, vb_s = _qkv(xs, nm0, wqkv, *tabs_s, tm=512)
    o_p = _attn_prompt(qt_p, kb_p, vt_p, w_subln, lam_p, out_scale)
    o_s = _attn_sample(q_s, kb_s, vb_s, cache_k[0].reshape(nb, past * N_HEADS, V_DIM),
                       cache_v[0].reshape(nb, past * N_HEADS, V_DIM), w_subln, lam_s, out_scale,
                       t=dec)

    nf0 = norm_ffn[0][None, :]
    wr0, br0 = _router_params(moe_w_router_group[0], moe_b_router_group[0],
                              moe_w_router_expert[0], moe_b_router_expert[0])
    wg0, wu0, wd0 = (moe_w_gate[0].astype(BF16), moe_w_up[0].astype(BF16),
                     moe_w_down[0].astype(BF16))
    nfin = norm_final[None, :]
    xp1 = _moe(xp, nf0, wr0, br0, wg0, wu0, wd0, nfin, False, tm=1024, out_rows=True,
               attn_proj=(o_p, wo))
    xs1 = _moe(xs, nf0, wr0, br0, wg0, wu0, wd0, nfin, False, tm=512, out_rows=True,
               attn_proj=(o_s, wo))

    a_re, a_im, bm, cm = _s5_params(ssm_lambda_re[0], ssm_lambda_im[0], ssm_log_dt[0],
                                    ssm_b_re[0], ssm_b_im[0], ssm_c_re[0], ssm_c_im[0])
    nm1 = norm_mix[1][None, :]
    nf1 = norm_ffn[1][None, :]
    wr1, br1 = _router_params(moe_w_router_group[1], moe_b_router_group[1],
                              moe_w_router_expert[1], moe_b_router_expert[1])
    s5w = (nm1, a_re, a_im, bm, cm, ssm_d[0][None, :], ssm_w_glu[0].astype(BF16),
           ssm_w_gate[0].astype(BF16))
    zero_state = jnp.zeros((SUBLANES, N_STATES), F32)
    x2_p, sr_p, si_p = _s5(xp1, zero_state, zero_state, *s5w, chain=True, steps=SSM_STEPS_PROMPT)
    x0r = state_ssm_re[0].reshape(nb, N_STATES)
    x0i = state_ssm_im[0].reshape(nb, N_STATES)
    x2_s, sr_s, si_s = _s5(xs1, x0r, x0i, *s5w, chain=False, steps=dec)

    wg1, wu1, wd1 = (moe_w_gate[1].astype(BF16), moe_w_up[1].astype(BF16),
                     moe_w_down[1].astype(BF16))
    y_p = _moe(x2_p, nf1, wr1, br1, wg1, wu1, wd1, nfin, True, tm=1024, in_rows=True)
    y_s = _moe(x2_s, nf1, wr1, br1, wg1, wu1, wd1, nfin, True, tm=512, in_rows=True)

    hshape = (N_HEADS, V_DIM)
    gshape = (N_SSM_GROUPS, SSM_STATE)
    return (y_p.reshape(1, seq, D_MODEL), y_s.reshape(nb, dec, D_MODEL),
            k_p.reshape(1, 1, seq, *hshape), v_p.reshape(1, 1, seq, *hshape),
            sr_p[SUBLANES - 1].reshape(1, 1, *gshape), si_p[SUBLANES - 1].reshape(1, 1, *gshape),
            k_s.reshape(1, nb, dec, *hshape), v_s.reshape(1, nb, dec, *hshape),
            sr_s.reshape(1, nb, *gshape), si_s.reshape(1, nb, *gshape))
```

```python
import functools
import math

import jax
import jax.numpy as jnp
import numpy as np
from jax import lax
from jax.experimental import pallas as pl
from jax.experimental.pallas import tpu as pltpu

F32 = jnp.float32
BF16 = jnp.bfloat16

D_MODEL = 1024
CHUNK = 64
N_HEADS = 8
HEAD_DIM = 64
V_DIM = 2 * HEAD_DIM
ROT_DIM = HEAD_DIM // 4
ROPE_THETA = 500000.0
SSM_GROUP = 16
N_SSM_GROUPS = D_MODEL // SSM_GROUP
SSM_STATE = 64
N_STATES = N_SSM_GROUPS * SSM_STATE
N_EXPERT_GROUPS = 4
EXPERTS_PER_GROUP = 4
N_EXPERTS = N_EXPERT_GROUPS * EXPERTS_PER_GROUP
D_EXPERT = D_MODEL // 4
RMS_EPS = 1e-6

LANES = 128
SUBLANES = 8
VMEM_LIMIT = 56 * 1024 * 1024

SSM_BLK_CH = 256
SSM_NBLK = D_MODEL // SSM_BLK_CH
SSM_BLK_ST = (SSM_BLK_CH // SSM_GROUP) * SSM_STATE
SSM_STEPS = 32
NLB = D_MODEL // LANES
SEG_PITCH = SSM_STEPS * NLB + SUBLANES


def _cparams(*sem, flags=None):
    return pltpu.CompilerParams(dimension_semantics=sem, vmem_limit_bytes=VMEM_LIMIT, flags=flags)


def _rms(x, w):
    return x * lax.rsqrt(jnp.mean(x * x, axis=-1, keepdims=True) + RMS_EPS) * w


def _qkv_kernel(x_ref, nw_ref, w_ref, cos_ref, sina_ref, sinb_ref,
                q_ref, k_ref, v_ref, kb_ref, vb_ref):
    h = _rms(x_ref[...], nw_ref[...]).astype(BF16)
    cosf, sina, sinb = cos_ref[...], sina_ref[...], sinb_ref[...]

    def rope(t):
        outs = []
        for hh in range(N_HEADS):
            xs = t[:, hh * LANES:(hh + 1) * LANES]
            outs.append(xs * cosf + pltpu.roll(xs, LANES - ROT_DIM // 2, 1) * sina
                        + pltpu.roll(xs, ROT_DIM // 2, 1) * sinb)
        return jnp.concatenate(outs, axis=1)

    q = rope(jnp.dot(h, w_ref[:, 0:D_MODEL], preferred_element_type=F32))
    q_ref[...] = (q * (HEAD_DIM ** -0.5)).astype(BF16)
    k = rope(jnp.dot(h, w_ref[:, D_MODEL:2 * D_MODEL], preferred_element_type=F32))
    k_ref[...] = k
    kb_ref[...] = k.astype(BF16)
    v = jnp.dot(h, w_ref[:, 2 * D_MODEL:3 * D_MODEL], preferred_element_type=F32)
    v_ref[...] = v
    vb_ref[...] = v.astype(BF16)


def _qkv(x, nw, w, cosf, sina, sinb, tm):
    n = x.shape[0]
    row = lambda i: (i, 0)
    fix = lambda i: (0, 0)
    return pl.pallas_call(
        _qkv_kernel,
        grid=(n // tm,),
        in_specs=[pl.BlockSpec((tm, D_MODEL), row), pl.BlockSpec((1, D_MODEL), fix),
                  pl.BlockSpec((D_MODEL, 3 * D_MODEL), fix),
                  pl.BlockSpec((tm, LANES), row), pl.BlockSpec((tm, LANES), row),
                  pl.BlockSpec((tm, LANES), row)],
        out_specs=[pl.BlockSpec((tm, D_MODEL), row)] * 5,
        out_shape=[jax.ShapeDtypeStruct((n, D_MODEL), BF16),
                   jax.ShapeDtypeStruct((n, D_MODEL), F32),
                   jax.ShapeDtypeStruct((n, D_MODEL), F32),
                   jax.ShapeDtypeStruct((n, D_MODEL), BF16),
                   jax.ShapeDtypeStruct((n, D_MODEL), BF16)],
        compiler_params=_cparams("arbitrary"),
        name="qkv_rope",
    )(x, nw, w, cosf, sina, sinb)


ATT_T = 512
ATT_K = 256
ATT_ONES = 16
ATT_UNROLL = 4
ATT_HEADS = 2
ATT_STRIP = 256


def _store_token_head_rows(ref, x):
    for r in range(x.shape[0] // SUBLANES):
        for hh in range(N_HEADS):
            ref[pl.ds(r * SUBLANES * N_HEADS + hh, SUBLANES, stride=N_HEADS), :] = (
                x[r * SUBLANES:(r + 1) * SUBLANES, hh * LANES:(hh + 1) * LANES])


def _seg_row(tok):
    return (tok // SSM_STEPS) * SEG_PITCH + (tok % SSM_STEPS) * NLB


def _store_segment_rows(ref, x):
    for r in range(x.shape[0] // SUBLANES):
        for b in range(NLB):
            ref[pl.ds(_seg_row(r * SUBLANES) + b, SUBLANES, stride=NLB), :] = (
                x[r * SUBLANES:(r + 1) * SUBLANES, b * LANES:(b + 1) * LANES])
    for s in range(x.shape[0] // SSM_STEPS):
        ref[pl.ds(s * SEG_PITCH + SSM_STEPS * NLB, SUBLANES), :] = jnp.zeros((SUBLANES, LANES), F32)


def _qkv_prompt_kernel(x_ref, nw_ref, wqt_ref, wk_ref, wv_ref,
                       cs_ref, expand_ref, cost_ref, sint_ref,
                       k_ref, v_ref, kb_ref, qt_ref, vt_ref):
    h = _rms(x_ref[...], nw_ref[...]).astype(BF16)
    tabs = jnp.dot(cs_ref[...], expand_ref[...], precision=lax.Precision.HIGHEST,
                   preferred_element_type=F32)
    lane = lax.broadcasted_iota(jnp.int32, (1, LANES), 1)
    cosf = tabs[:, 0:LANES] + jnp.where(lane % HEAD_DIM >= ROT_DIM, 1.0, 0.0)
    sina, sinb = tabs[:, LANES:2 * LANES], tabs[:, 2 * LANES:3 * LANES]
    outs = []
    k = jnp.dot(h, wk_ref[...], preferred_element_type=F32)
    for hh in range(N_HEADS):
        xs = k[:, hh * LANES:(hh + 1) * LANES]
        outs.append(xs * cosf + pltpu.roll(xs, LANES - ROT_DIM // 2, 1) * sina
                    + pltpu.roll(xs, ROT_DIM // 2, 1) * sinb)
    k = jnp.concatenate(outs, axis=1)
    _store_token_head_rows(k_ref, k)
    kb_ref[...] = k.astype(BF16)
    v = jnp.dot(h, wv_ref[...], preferred_element_type=F32)
    _store_token_head_rows(v_ref, v)
    vt = v.T.astype(BF16)

    nt = (((1,), (1,)), ((), ()))
    qt = lax.dot_general(wqt_ref[...], h, nt, preferred_element_type=F32) * (
        HEAD_DIM ** -0.5 * math.log2(math.e))
    cost, sint = cost_ref[...], sint_ref[...]
    half = ROT_DIM // 2
    pieces = []
    for blk in range(D_MODEL // HEAD_DIM):
        r0 = blk * HEAD_DIM
        x1, x2 = qt[r0:r0 + half], qt[r0 + half:r0 + ROT_DIM]
        pieces += [x1 * cost - x2 * sint, x2 * cost + x1 * sint, qt[r0 + ROT_DIM:r0 + HEAD_DIM]]
    qt = jnp.concatenate(pieces, axis=0).astype(BF16)
    for b in range(qt_ref.shape[0]):
        qt_ref[b] = qt[:, b * ATT_T:(b + 1) * ATT_T]
    for b in range(vt_ref.shape[0]):
        vt_ref[b] = vt[:, b * ATT_K:(b + 1) * ATT_K]


def _qkv_prompt(x, nw, wqt, wk, wv, cs, expand, cost, sint, tm):
    n = x.shape[0]
    row = lambda i: (i, 0)
    fix = lambda i: (0, 0)
    wspec = pl.BlockSpec((D_MODEL, D_MODEL), fix)
    ttspec = pl.BlockSpec((ROT_DIM // 2, tm), lambda i: (0, i))
    thspec = pl.BlockSpec((tm * N_HEADS, V_DIM), row)
    qblk = pl.BlockSpec((tm // ATT_T, D_MODEL, ATT_T), lambda i: (i, 0, 0))
    vblk = pl.BlockSpec((tm // ATT_K, D_MODEL, ATT_K), lambda i: (i, 0, 0))
    return pl.pallas_call(
        _qkv_prompt_kernel,
        grid=(n // tm,),
        in_specs=[pl.BlockSpec((tm, D_MODEL), row), pl.BlockSpec((1, D_MODEL), fix),
                  wspec, wspec, wspec,
                  pl.BlockSpec((tm, ROT_DIM), row), pl.BlockSpec((ROT_DIM, 3 * LANES), fix),
                  ttspec, ttspec],
        out_specs=[thspec, thspec, pl.BlockSpec((tm, D_MODEL), row), qblk, vblk],
        out_shape=[jax.ShapeDtypeStruct((n * N_HEADS, V_DIM), F32),
                   jax.ShapeDtypeStruct((n * N_HEADS, V_DIM), F32),
                   jax.ShapeDtypeStruct((n, D_MODEL), BF16),
                   jax.ShapeDtypeStruct((n // ATT_T, D_MODEL, ATT_T), BF16),
                   jax.ShapeDtypeStruct((n // ATT_K, D_MODEL, ATT_K), BF16)],
        compiler_params=_cparams("arbitrary"),
        name="qkv_rope_prompt",
    )(x, nw, wqt, wk, wv, cs, expand, cost, sint)


def _stack_subheads(q):
    lane = lax.broadcasted_iota(jnp.int32, q.shape, 1)
    zero = jnp.zeros_like(q)
    return jnp.concatenate([jnp.where(lane < HEAD_DIM, q, zero),
                            jnp.where(lane >= HEAD_DIM, q, zero)], axis=0)


def _scores(qs, k):
    return lax.dot_general(qs, k, (((1,), (1,)), ((), ())), preferred_element_type=F32)


def _diff_finish(acc, l, t, lam, w, out_scale):
    o = acc[:t] / l[:t] - lam * (acc[t:] / l[t:])
    return _rms(o, w) * out_scale


def _attn_prompt_kernel(qt_ref, k_ref, vt_ref, w_ref, lam_ref, o_ref, sa_ref, sb_ref, acc_ref,
                        bias_ref, *, out_scale):
    t, tk = ATT_T, ATT_K
    heads = range(ATT_HEADS)
    i = pl.program_id(1)
    row = lax.broadcasted_iota(jnp.int32, (V_DIM, t), 0)
    qs = []
    for g in heads:
        qt = qt_ref[0, g * V_DIM:(g + 1) * V_DIM, :]
        zero = jnp.zeros_like(qt)
        qs.append(jnp.concatenate([jnp.where(row < HEAD_DIM, qt, zero),
                                   jnp.where(row >= HEAD_DIM, qt, zero)], axis=1))

    all_strips = tuple(range(2 * t // ATT_STRIP))
    late_strips = tuple(c for c in all_strips if (c * ATT_STRIP) % t + ATT_STRIP > t - tk)

    def scores(ref, g, j, bias=None, strips=all_strips):
        r0 = pl.multiple_of(j * tk, tk)
        kblk = k_ref[pl.ds(r0, tk), g * V_DIM:(g + 1) * V_DIM]
        smax = []
        for c in all_strips:
            if c not in strips:
                smax.append(jnp.full((1, ATT_STRIP), -jnp.inf, F32))
                continue
            cols = slice(c * ATT_STRIP, (c + 1) * ATT_STRIP)
            s = jnp.dot(kblk, qs[g][:, cols], preferred_element_type=F32)
            if bias is not None:
                s = s + bias[:, cols]
            ref[g, :, cols] = s
            smax.append(jnp.max(s, axis=0, keepdims=True))
        return jnp.concatenate(smax, axis=1)

    ones = jnp.ones((ATT_ONES, tk), BF16)

    def update(g, m, ref, smax, j, strips=all_strips):
        m_new = jnp.maximum(m, smax)
        alpha = jnp.exp2(m - m_new)
        v1 = jnp.concatenate([vt_ref[j, g * V_DIM:(g + 1) * V_DIM, :], ones], axis=0)
        for c in strips:
            cols = slice(c * ATT_STRIP, (c + 1) * ATT_STRIP)
            p = jnp.exp2(ref[g, :, cols] - m_new[:, cols]).astype(BF16)
            acc_ref[g, :, cols] = (alpha[:, cols] * acc_ref[g, :, cols]
                                   + jnp.dot(v1, p, preferred_element_type=F32))
        return m_new

    @pl.when((pl.program_id(0) == 0) & (i == 0))
    def _():
        for b in range(t // tk):
            kc = (b * tk + lax.broadcasted_iota(jnp.int32, (tk, 2 * t), 0)) // CHUNK
            qc = (lax.broadcasted_iota(jnp.int32, (tk, 2 * t), 1) % t) // CHUNK
            bias_ref[b] = jnp.where(kc <= qc, 0.0, -jnp.inf)

    acc_ref[...] = jnp.zeros(acc_ref.shape, F32)
    first_bias = jnp.where(i == 0, bias_ref[0], 0.0)
    xa = tuple(scores(sa_ref, g, 0, first_bias) for g in heads)

    def pairs(n, last_bias=None):
        def body(jj, carry):
            ms, xa = list(carry[0]), list(carry[1])
            for u in range(n):
                j = 2 * (jj * n + u)
                xb = [scores(sb_ref, g, j + 1) for g in heads]
                for g in heads:
                    ms[g] = update(g, ms[g], sa_ref, xa[g], j)
                for g in heads:
                    xa[g] = scores(sa_ref, g, j + 2, last_bias)
                for g in heads:
                    ms[g] = update(g, ms[g], sb_ref, xb[g], j + 1)
            return tuple(ms), tuple(xa)
        return body

    ms = tuple(jnp.full((1, 2 * t), -jnp.inf, F32) for _ in heads)
    nfree = jnp.maximum(i - 1, 0)
    nlong = nfree // ATT_UNROLL
    carry = lax.fori_loop(0, nlong, pairs(ATT_UNROLL), (ms, xa))
    carry = lax.fori_loop(nlong * ATT_UNROLL, nfree, pairs(1), carry)
    ms, xa = lax.fori_loop(nfree, i, pairs(1, bias_ref[0]), carry)

    xb = [scores(sb_ref, g, 2 * i + 1, bias_ref[1], late_strips) for g in heads]
    ms = [update(g, ms[g], sa_ref, xa[g], 2 * i) for g in heads]
    for g in heads:
        update(g, ms[g], sb_ref, xb[g], 2 * i + 1, late_strips)

    for g in heads:
        acc = acc_ref[g, 0:V_DIM, :]
        l = acc_ref[g, V_DIM:V_DIM + 1, :]
        o = acc[:, :t] / l[:, :t] - lam_ref[...] * (acc[:, t:] / l[:, t:])
        o = o * lax.rsqrt(jnp.mean(o * o, axis=0, keepdims=True) + RMS_EPS)
        o_ref[:, g * V_DIM:(g + 1) * V_DIM] = (o.T * w_ref[...] * out_scale).astype(BF16)


def _attn_prompt(qt, kb, vt, w_subln, lam, out_scale):
    l = kb.shape[0]
    t, tk, ng = ATT_T, ATT_K, ATT_HEADS
    kern = functools.partial(_attn_prompt_kernel, out_scale=out_scale)
    return pl.pallas_call(
        kern,
        grid=(N_HEADS // ng, l // t),
        in_specs=[pl.BlockSpec((1, ng * V_DIM, t), lambda h, i: (i, h, 0)),
                  pl.BlockSpec((l, ng * V_DIM), lambda h, i: (0, h)),
                  pl.BlockSpec((l // tk, ng * V_DIM, tk), lambda h, i: (0, h, 0)),
                  pl.BlockSpec((1, V_DIM), lambda h, i: (0, 0)),
                  pl.BlockSpec((1, t), lambda h, i: (0, 0))],
        out_specs=pl.BlockSpec((t, ng * V_DIM), lambda h, i: (i, h)),
        out_shape=jax.ShapeDtypeStruct((l, D_MODEL), BF16),
        scratch_shapes=[pltpu.VMEM((ng, tk, 2 * t), F32), pltpu.VMEM((ng, tk, 2 * t), F32),
                        pltpu.VMEM((ng, V_DIM + ATT_ONES, 2 * t), F32),
                        pltpu.VMEM((t // tk, tk, 2 * t), F32)],
        compiler_params=_cparams("arbitrary", "arbitrary"),
        name="attn_prompt",
    )(qt, kb, vt, w_subln, lam)


def _attn_sample_kernel(q_ref, kn_ref, vn_ref, ck_ref, cv_ref, w_ref, lam_ref, o_ref,
                        *, t, out_scale):
    for hh in range(N_HEADS):
        cols = slice(hh * V_DIM, (hh + 1) * V_DIM)
        qs = _stack_subheads(q_ref[:, cols])
        past = ck_ref.shape[1] // N_HEADS
        kc = ck_ref[0, pl.ds(hh, past, stride=N_HEADS), :].astype(BF16)
        vc = cv_ref[0, pl.ds(hh, past, stride=N_HEADS), :].astype(BF16)
        s_c = _scores(qs, kc)
        s_n = _scores(qs, kn_ref[:, cols])
        m = jnp.maximum(jnp.max(s_c, axis=-1, keepdims=True), jnp.max(s_n, axis=-1, keepdims=True))
        p_c = jnp.exp(s_c - m)
        p_n = jnp.exp(s_n - m)
        l = jnp.sum(p_c, axis=-1, keepdims=True) + jnp.sum(p_n, axis=-1, keepdims=True)
        acc = (jnp.dot(p_c.astype(BF16), vc, preferred_element_type=F32)
               + jnp.dot(p_n.astype(BF16), vn_ref[:, cols], preferred_element_type=F32))
        o_ref[:, cols] = _diff_finish(acc, l, t, lam_ref[...], w_ref[...], out_scale).astype(BF16)


def _attn_sample(q, kb, vb, cache_k, cache_v, w_subln, lam, out_scale, t):
    nb, rows, _ = cache_k.shape
    kern = functools.partial(_attn_sample_kernel, t=t, out_scale=out_scale)
    row = lambda b: (b, 0)
    return pl.pallas_call(
        kern,
        grid=(nb,),
        in_specs=[pl.BlockSpec((t, D_MODEL), row), pl.BlockSpec((t, D_MODEL), row),
                  pl.BlockSpec((t, D_MODEL), row),
                  pl.BlockSpec((1, rows, V_DIM), lambda b: (b, 0, 0)),
                  pl.BlockSpec((1, rows, V_DIM), lambda b: (b, 0, 0)),
                  pl.BlockSpec((1, V_DIM), lambda b: (0, 0)),
                  pl.BlockSpec((1, V_DIM), lambda b: (0, 0))],
        out_specs=pl.BlockSpec((t, D_MODEL), row),
        out_shape=jax.ShapeDtypeStruct((nb * t, D_MODEL), BF16),
        compiler_params=_cparams("arbitrary"),
        name="attn_sample",
    )(q, kb, vb, cache_k, cache_v, w_subln, lam)


def _route(x, nw, wr, br):
    hf = _rms(x, nw)
    h = hf.astype(BF16)
    h_lo = (hf - h.astype(F32)).astype(BF16)
    hw = jnp.dot(h, wr, preferred_element_type=F32)
    lg = (hw[:, :LANES] + hw[:, LANES:] + jnp.dot(h_lo, wr[:, :LANES], preferred_element_type=F32)
          + br)
    lane = lax.broadcasted_iota(jnp.int32, lg.shape, 1)
    big = jnp.int32(LANES)
    neg = jnp.float32(-jnp.inf)

    def masked_softmax(mask):
        z = jnp.where(mask, lg, neg)
        e = jnp.exp(z - jnp.max(z, axis=-1, keepdims=True))
        return e / jnp.sum(e, axis=-1, keepdims=True)

    def top1(p, mask):
        v = jnp.max(jnp.where(mask, p, -1.0), axis=-1, keepdims=True)
        idx = jnp.min(jnp.where(mask & (p == v), lane, big), axis=-1, keepdims=True)
        return v, idx

    gmask = lane < N_EXPERT_GROUPS
    g_val, g_idx = top1(masked_softmax(gmask), gmask)
    lo = N_EXPERT_GROUPS + g_idx * EXPERTS_PER_GROUP
    emask = (lane >= lo) & (lane < lo + EXPERTS_PER_GROUP)
    pe = masked_softmax(emask)
    v1, i1 = top1(pe, emask)
    v2, i2 = top1(pe, emask & (lane != i1))
    den = v1 + v2
    w1 = g_val * v1 / den
    w2 = g_val * v2 / den
    comb = (jnp.where(lane == i1 - N_EXPERT_GROUPS, w1, 0.0)
            + jnp.where(lane == i2 - N_EXPERT_GROUPS, w2, 0.0))
    return h, comb


def _moe_kernel(*refs, final_norm, in_rows, out_rows, attn_proj):
    if attn_proj:
        x_ref, o_ref, wo_ref = refs[:3]
        refs = refs[3:]
    else:
        x_ref, refs = refs[0], refs[1:]
    (nfw_ref, wr_ref, br_ref, wg_ref, wu_ref, wd_ref, nw_ref, y_ref,
     h_ref, c_ref, acc_ref) = refs
    g = pl.program_id(1)
    nlb = D_MODEL // LANES

    @pl.when(g == 0)
    def _():
        if in_rows:
            for r in range(acc_ref.shape[0] // SUBLANES):
                for b in range(nlb):
                    acc_ref[r * SUBLANES:(r + 1) * SUBLANES, b * LANES:(b + 1) * LANES] = (
                        x_ref[pl.ds(_seg_row(r * SUBLANES) + b, SUBLANES, stride=nlb), :])
        elif attn_proj:
            acc_ref[...] = x_ref[...] + jnp.dot(o_ref[...], wo_ref[...],
                                                preferred_element_type=F32)
        else:
            acc_ref[...] = x_ref[...]
        h_ref[...], c_ref[...] = _route(acc_ref[...], nfw_ref[...], wr_ref[...], br_ref[...])

    h = h_ref[...]
    comb = c_ref[...]
    acc = acc_ref[...]
    for e in range(EXPERTS_PER_GROUP):
        gate = jnp.dot(h, wg_ref[e], preferred_element_type=F32)
        up = jnp.dot(h, wu_ref[e], preferred_element_type=F32)
        hid = (gate * jax.nn.sigmoid(gate) * up).astype(BF16)
        lane = lax.broadcasted_iota(jnp.int32, comb.shape, 1)
        ce = jnp.sum(jnp.where(lane == g * EXPERTS_PER_GROUP + e, comb, 0.0),
                     axis=-1, keepdims=True)
        acc = acc + ce * jnp.dot(hid, wd_ref[e], preferred_element_type=F32)
    acc_ref[...] = acc

    @pl.when(g == N_EXPERT_GROUPS - 1)
    def _():
        y = acc_ref[...]
        if final_norm:
            y = _rms(y, nw_ref[...])
        if out_rows:
            _store_segment_rows(y_ref, y)
        else:
            y_ref[...] = y


def _moe(x, nfw, wr, br, wg, wu, wd, nw, final_norm, tm, in_rows=False, out_rows=False,
         attn_proj=None):
    n = x.shape[0] // SEG_PITCH * SSM_STEPS if in_rows else x.shape[0]
    row = lambda i, g: (i, 0)
    fix = lambda i, g: (0, 0)
    rows_spec = pl.BlockSpec((tm // SSM_STEPS * SEG_PITCH, LANES), row)
    nat_spec = pl.BlockSpec((tm, D_MODEL), row)
    kern = functools.partial(_moe_kernel, final_norm=final_norm, in_rows=in_rows,
                             out_rows=out_rows, attn_proj=attn_proj is not None)
    proj_specs = [nat_spec, pl.BlockSpec((D_MODEL, D_MODEL), fix)] if attn_proj else []
    return pl.pallas_call(
        kern,
        grid=(n // tm, N_EXPERT_GROUPS),
        in_specs=[rows_spec if in_rows else nat_spec] + proj_specs + [
                  pl.BlockSpec((1, D_MODEL), fix),
                  pl.BlockSpec((D_MODEL, 2 * LANES), fix), pl.BlockSpec((1, LANES), fix),
                  pl.BlockSpec((EXPERTS_PER_GROUP, D_MODEL, D_EXPERT), lambda i, g: (g, 0, 0)),
                  pl.BlockSpec((EXPERTS_PER_GROUP, D_MODEL, D_EXPERT), lambda i, g: (g, 0, 0)),
                  pl.BlockSpec((EXPERTS_PER_GROUP, D_EXPERT, D_MODEL), lambda i, g: (g, 0, 0)),
                  pl.BlockSpec((1, D_MODEL), lambda i, g: (0, 0))],
        out_specs=rows_spec if out_rows else nat_spec,
        out_shape=jax.ShapeDtypeStruct(
            (n // SSM_STEPS * SEG_PITCH, LANES) if out_rows else (n, D_MODEL), F32),
        scratch_shapes=[pltpu.VMEM((tm, D_MODEL), BF16), pltpu.VMEM((tm, LANES), F32),
                        pltpu.VMEM((tm, D_MODEL), F32)],
        compiler_params=_cparams("arbitrary", "arbitrary"),
        name="moe",
    )(x, *(attn_proj or ()), nfw, wr, br, wg, wu, wd, nw)


def _cmul(ar, ai, br, bi):
    return ar * br - ai * bi, ar * bi + ai * br


def _s5_kernel(x_ref, x0r_ref, x0i_ref, nmw_ref, ar_ref, ai_ref, bm_ref, cm_ref, d_ref,
               wglu_ref, wgate_ref,
               x2_ref, sr_ref, si_ref,
               xp_ref, bu_all_ref, y_ref, pr_ref, pi_ref, cr_ref, ci_ref, car_r_ref, car_i_ref,
               *, chain, steps):
    rows, nst = SUBLANES * steps, SSM_BLK_ST

    @pl.when(pl.program_id(0) == 0)
    def _():
        for kb in range(SSM_NBLK):
            ar, ai = ar_ref[kb], ai_ref[kb]
            pr, pi = ar, ai
            for i in range(steps):
                if i:
                    pr, pi = _cmul(ar, ai, pr, pi)
                pr_ref[kb, i * SUBLANES:(i + 1) * SUBLANES, :] = jnp.broadcast_to(pr, (SUBLANES, nst))
                pi_ref[kb, i * SUBLANES:(i + 1) * SUBLANES, :] = jnp.broadcast_to(pi, (SUBLANES, nst))
        car_r_ref[...] = jnp.zeros_like(car_r_ref)
        car_i_ref[...] = jnp.zeros_like(car_i_ref)

    nlb = NLB
    for i in range(steps):
        for b in range(nlb):
            xp_ref[i * SUBLANES:(i + 1) * SUBLANES, b * LANES:(b + 1) * LANES] = (
                x_ref[pl.ds(i * nlb + b, SUBLANES, stride=SEG_PITCH), :])

    u = _rms(xp_ref[...], nmw_ref[...])
    ub = u.astype(BF16)

    half = nst // 2
    for kb in range(SSM_NBLK):
        bu_ref = bu_all_ref.at[kb % bu_all_ref.shape[0]]
        bu_ref[...] = jnp.dot(ub[:, kb * SSM_BLK_CH:(kb + 1) * SSM_BLK_CH], bm_ref[kb],
                              preferred_element_type=F32)
        for hh in range(2):
            re = slice(hh * half, (hh + 1) * half)
            im = slice(nst + hh * half, nst + (hh + 1) * half)
            ar = jnp.broadcast_to(ar_ref[kb, :, re], (SUBLANES, half))
            ai = jnp.broadcast_to(ai_ref[kb, :, re], (SUBLANES, half))

            def scan_step(i, s, re=re, im=im, ar=ar, ai=ai, bu_ref=bu_ref):
                sr, si = s
                r0 = pl.multiple_of(i * SUBLANES, SUBLANES)
                tr, ti = _cmul(ar, ai, sr, si)
                sr = tr + bu_ref[pl.ds(r0, SUBLANES), re]
                si = ti + bu_ref[pl.ds(r0, SUBLANES), im]
                bu_ref[pl.ds(r0, SUBLANES), re] = sr
                bu_ref[pl.ds(r0, SUBLANES), im] = si
                return sr, si

            zero = jnp.zeros((SUBLANES, half), F32)
            lax.fori_loop(0, steps, scan_step, (zero, zero), unroll=True)

        if chain:
            alr, ali = pr_ref[kb, rows - 1:rows, :], pi_ref[kb, rows - 1:rows, :]
            cr, ci = car_r_ref[kb], car_i_ref[kb]
            for j in range(SUBLANES):
                cr_ref[j:j + 1, :] = cr
                ci_ref[j:j + 1, :] = ci
                last = rows - SUBLANES + j
                tr, ti = _cmul(alr, ali, cr, ci)
                cr = tr + bu_ref[last:last + 1, 0:nst]
                ci = ti + bu_ref[last:last + 1, nst:2 * nst]
            car_r_ref[kb] = cr
            car_i_ref[kb] = ci
        else:
            cr_ref[...] = x0r_ref[:, kb * nst:(kb + 1) * nst]
            ci_ref[...] = x0i_ref[:, kb * nst:(kb + 1) * nst]

        c_r, c_i = cr_ref[...], ci_ref[...]

        def fix_step(i, _, kb=kb, c_r=c_r, c_i=c_i, bu_ref=bu_ref):
            r0 = pl.multiple_of(i * SUBLANES, SUBLANES)
            tr, ti = _cmul(pr_ref[kb, pl.ds(r0, SUBLANES), :], pi_ref[kb, pl.ds(r0, SUBLANES), :],
                           c_r, c_i)
            bu_ref[pl.ds(r0, SUBLANES), 0:nst] = bu_ref[pl.ds(r0, SUBLANES), 0:nst] + tr
            bu_ref[pl.ds(r0, SUBLANES), nst:2 * nst] = bu_ref[pl.ds(r0, SUBLANES), nst:2 * nst] + ti
            return 0

        lax.fori_loop(0, steps, fix_step, 0, unroll=True)
        sr_ref[:, kb * nst:(kb + 1) * nst] = bu_ref[rows - SUBLANES:rows, 0:nst]
        si_ref[:, kb * nst:(kb + 1) * nst] = bu_ref[rows - SUBLANES:rows, nst:2 * nst]
        y_ref[:, kb * SSM_BLK_CH:(kb + 1) * SSM_BLK_CH] = jnp.dot(
            bu_ref[...].astype(BF16), cm_ref[kb], preferred_element_type=F32)

    y = y_ref[...] + d_ref[...] * u
    z = jax.nn.gelu(y).astype(BF16)
    out = (jnp.dot(z, wglu_ref[...], preferred_element_type=F32)
           * jax.nn.sigmoid(jnp.dot(z, wgate_ref[...], preferred_element_type=F32)))
    y_ref[...] = xp_ref[...] + out
    for i in range(steps):
        for b in range(nlb):
            x2_ref[pl.ds(i * nlb + b, SUBLANES, stride=SEG_PITCH), :] = (
                y_ref[i * SUBLANES:(i + 1) * SUBLANES, b * LANES:(b + 1) * LANES])
    for j in range(SUBLANES):
        x2_ref[pl.ds(j * SEG_PITCH + steps * nlb, SUBLANES), :] = jnp.zeros((SUBLANES, LANES), F32)


def _s5(x, x0r, x0i, nmw, a_re, a_im, bm, cm, d, wglu, wgate, chain, steps):
    assert steps == SSM_STEPS
    n = x.shape[0] // SEG_PITCH * SSM_STEPS
    rows = SUBLANES * steps
    nchunk = n // rows
    row = lambda i: (i, 0)
    fix2 = lambda i: (0, 0)
    fix3 = lambda i: (0, 0, 0)
    state_map = fix2 if chain else row
    n_state_rows = SUBLANES if chain else SUBLANES * nchunk
    kern = functools.partial(_s5_kernel, chain=chain, steps=steps)
    return pl.pallas_call(
        kern,
        grid=(nchunk,),
        in_specs=[pl.BlockSpec((SUBLANES * SEG_PITCH, LANES), row),
                  pl.BlockSpec((SUBLANES, N_STATES), state_map),
                  pl.BlockSpec((SUBLANES, N_STATES), state_map),
                  pl.BlockSpec((1, D_MODEL), fix2),
                  pl.BlockSpec((SSM_NBLK, 1, SSM_BLK_ST), fix3),
                  pl.BlockSpec((SSM_NBLK, 1, SSM_BLK_ST), fix3),
                  pl.BlockSpec((SSM_NBLK, SSM_BLK_CH, 2 * SSM_BLK_ST), fix3),
                  pl.BlockSpec((SSM_NBLK, 2 * SSM_BLK_ST, SSM_BLK_CH), fix3),
                  pl.BlockSpec((1, D_MODEL), fix2),
                  pl.BlockSpec((D_MODEL, D_MODEL), fix2),
                  pl.BlockSpec((D_MODEL, D_MODEL), fix2)],
        out_specs=[pl.BlockSpec((SUBLANES * SEG_PITCH, LANES), row),
                   pl.BlockSpec((SUBLANES, N_STATES), state_map),
                   pl.BlockSpec((SUBLANES, N_STATES), state_map)],
        out_shape=[jax.ShapeDtypeStruct((n // SSM_STEPS * SEG_PITCH, LANES), F32),
                   jax.ShapeDtypeStruct((n_state_rows, N_STATES), F32),
                   jax.ShapeDtypeStruct((n_state_rows, N_STATES), F32)],
        scratch_shapes=[pltpu.VMEM((rows, D_MODEL), F32),
                        pltpu.VMEM((1, rows, 2 * SSM_BLK_ST), F32),
                        pltpu.VMEM((rows, D_MODEL), F32),
                        pltpu.VMEM((SSM_NBLK, rows, SSM_BLK_ST), F32),
                        pltpu.VMEM((SSM_NBLK, rows, SSM_BLK_ST), F32),
                        pltpu.VMEM((SUBLANES, SSM_BLK_ST), F32),
                        pltpu.VMEM((SUBLANES, SSM_BLK_ST), F32),
                        pltpu.VMEM((SSM_NBLK, 1, SSM_BLK_ST), F32),
                        pltpu.VMEM((SSM_NBLK, 1, SSM_BLK_ST), F32)],
        compiler_params=_cparams("arbitrary"),
        name="s5_mixer",
    )(x, x0r, x0i, nmw, a_re, a_im, bm, cm, d, wglu, wgate)


def _rope_cos_sin(pos):
    half = ROT_DIM // 2
    inv_freq = ROPE_THETA ** (-jnp.arange(half, dtype=F32) * 2.0 / ROT_DIM)
    ang = pos.astype(F32)[:, None] * inv_freq[None, :]
    return jnp.cos(ang), jnp.sin(ang)


def _rope_expand_matrix():
    half = ROT_DIM // 2
    e = np.zeros((ROT_DIM, 3 * LANES), np.float32)
    for lane in range(LANES):
        d = lane % HEAD_DIM
        if d < ROT_DIM:
            e[d % half, lane] = 1.0
        if d < half:
            e[half + d, LANES + lane] = -1.0
        elif d < ROT_DIM:
            e[half + d - half, 2 * LANES + lane] = 1.0
    return jnp.asarray(e)


def _rope_tables(pos):
    half = ROT_DIM // 2
    cos, sin = _rope_cos_sin(pos)
    n = pos.shape[0]
    pad = HEAD_DIM - ROT_DIM
    cosf = jnp.concatenate([cos, cos, jnp.ones((n, pad), F32)], axis=1)
    sina = jnp.concatenate([-sin, jnp.zeros((n, half + pad), F32)], axis=1)
    sinb = jnp.concatenate([jnp.zeros((n, half), F32), sin, jnp.zeros((n, pad), F32)], axis=1)
    tile2 = lambda t: jnp.concatenate([t, t], axis=1)
    return tile2(cosf), tile2(sina), tile2(sinb)


def _s5_params(lam_re, lam_im, log_dt, b_re, b_im, c_re, c_im):
    dt = jnp.exp(log_dt)[:, None]
    z_re, z_im = lam_re * dt, lam_im * dt
    mag = jnp.exp(z_re)
    lb_re, lb_im = mag * jnp.cos(z_im), mag * jnp.sin(z_im)
    n_re, n_im = lb_re - 1.0, lb_im
    den = lam_re * lam_re + lam_im * lam_im
    k_re = (n_re * lam_re + n_im * lam_im) / den
    k_im = (n_im * lam_re - n_re * lam_im) / den
    bb_re = k_re[..., None] * b_re - k_im[..., None] * b_im
    bb_im = k_re[..., None] * b_im + k_im[..., None] * b_re
    gpb = SSM_BLK_CH // SSM_GROUP
    eye = jnp.eye(gpb, dtype=F32)

    def in_blocks(bb):
        v = bb.reshape(SSM_NBLK, gpb, SSM_STATE, SSM_GROUP).transpose(0, 1, 3, 2)
        return jnp.einsum('kgcp,gh->kgchp', v, eye).reshape(SSM_NBLK, SSM_BLK_CH, SSM_BLK_ST)

    def out_blocks(cc):
        v = cc.reshape(SSM_NBLK, gpb, SSM_GROUP, SSM_STATE).transpose(0, 1, 3, 2)
        return jnp.einsum('kgpc,gh->kgphc', v, eye).reshape(SSM_NBLK, SSM_BLK_ST, SSM_BLK_CH)

    bm = jnp.concatenate([in_blocks(bb_re), in_blocks(bb_im)], axis=2).astype(BF16)
    cm = jnp.concatenate([out_blocks(c_re), -out_blocks(c_im)], axis=1).astype(BF16)
    a_re = lb_re.reshape(SSM_NBLK, 1, SSM_BLK_ST)
    a_im = lb_im.reshape(SSM_NBLK, 1, SSM_BLK_ST)
    return a_re, a_im, bm, cm


def _router_params(w_rg, b_rg, w_re, b_re):
    pad = LANES - N_EXPERT_GROUPS - N_EXPERTS
    wr = jnp.concatenate([w_rg, w_re, jnp.zeros((D_MODEL, pad), F32)], axis=1)
    w_hi = wr.astype(BF16)
    wr = jnp.concatenate([w_hi, (wr - w_hi.astype(F32)).astype(BF16)], axis=1)
    br = jnp.concatenate([b_rg, b_re, jnp.zeros((pad,), F32)])[None, :]
    return wr, br


def kernel(x_prompt, x_sample, cache_k, cache_v, state_ssm_re, state_ssm_im, norm_mix, norm_ffn, norm_final, attn_w_qkv, attn_lambda_q1, attn_lambda_k1, attn_lambda_q2, attn_lambda_k2, attn_subln, attn_w_o, ssm_lambda_re, ssm_lambda_im, ssm_log_dt, ssm_b_re, ssm_b_im, ssm_c_re, ssm_c_im, ssm_d, ssm_w_glu, ssm_w_gate, moe_w_router_group, moe_b_router_group, moe_w_router_expert, moe_b_router_expert, moe_w_gate, moe_w_up, moe_w_down):
    _, seq, _ = x_prompt.shape
    nb, dec, _ = x_sample.shape
    past = cache_k.shape[2]
    xp = x_prompt.reshape(seq, D_MODEL)
    xs = x_sample.reshape(nb * dec, D_MODEL)

    lam_init = 0.8 - 0.6 * math.exp(-0.3 * 0)
    lam = (jnp.exp(jnp.sum(attn_lambda_q1[0] * attn_lambda_k1[0]))
           - jnp.exp(jnp.sum(attn_lambda_q2[0] * attn_lambda_k2[0])) + lam_init)
    lam_p = jnp.full((1, ATT_T), lam, F32)
    lam_s = jnp.full((1, V_DIM), lam, F32)
    w_subln = attn_subln[0][None, :]
    out_scale = 1.0 - lam_init
    wqkv = attn_w_qkv[0].astype(BF16)
    wq, wk, wv = (wqkv[:, j * D_MODEL:(j + 1) * D_MODEL] for j in range(3))
    wo = attn_w_o[0].astype(BF16)
    nm0 = norm_mix[0][None, :]
    cos_p, sin_p = _rope_cos_sin(jnp.arange(seq, dtype=jnp.int32))
    tabs_s = tuple(jnp.tile(t, (nb, 1))
                   for t in _rope_tables(past + jnp.arange(dec, dtype=jnp.int32)))

    k_p, v_p, kb_p, qt_p, vt_p = _qkv_prompt(
        xp, nm0, wq.T, wk, wv, jnp.concatenate([cos_p, sin_p], axis=1),
        _rope_expand_matrix(), cos_p.T, sin_p.T, tm=512)
    q_s, k_s, v_s, kb_s, vb_s = _qkv(xs, nm0, wqkv, *tabs_s, tm=512)
    o_p = _attn_prompt(qt_p, kb_p, vt_p, w_subln, lam_p, out_scale)
    o_s = _attn_sample(q_s, kb_s, vb_s, cache_k[0].reshape(nb, past * N_HEADS, V_DIM),
                       cache_v[0].reshape(nb, past * N_HEADS, V_DIM), w_subln, lam_s, out_scale,
                       t=dec)

    nf0 = norm_ffn[0][None, :]
    wr0, br0 = _router_params(moe_w_router_group[0], moe_b_router_group[0],
                              moe_w_router_expert[0], moe_b_router_expert[0])
    wg0, wu0, wd0 = (moe_w_gate[0].astype(BF16), moe_w_up[0].astype(BF16),
                     moe_w_down[0].astype(BF16))
    nfin = norm_final[None, :]
    xp1 = _moe(xp, nf0, wr0, br0, wg0, wu0, wd0, nfin, False, tm=1024, out_rows=True,
               attn_proj=(o_p, wo))
    xs1 = _moe(xs, nf0, wr0, br0, wg0, wu0, wd0, nfin, False, tm=512, out_rows=True,
               attn_proj=(o_s, wo))

    a_re, a_im, bm, cm = _s5_params(ssm_lambda_re[0], ssm_lambda_im[0], ssm_log_dt[0],
                                    ssm_b_re[0], ssm_b_im[0], ssm_c_re[0], ssm_c_im[0])
    nm1 = norm_mix[1][None, :]
    nf1 = norm_ffn[1][None, :]
    wr1, br1 = _router_params(moe_w_router_group[1], moe_b_router_group[1],
                              moe_w_router_expert[1], moe_b_router_expert[1])
    s5w = (nm1, a_re, a_im, bm, cm, ssm_d[0][None, :], ssm_w_glu[0].astype(BF16),
           ssm_w_gate[0].astype(BF16))
    zero_state = jnp.zeros((SUBLANES, N_STATES), F32)
    x2_p, sr_p, si_p = _s5(xp1, zero_state, zero_state, *s5w, chain=True, steps=SSM_STEPS)
    x0r = state_ssm_re[0].reshape(nb, N_STATES)
    x0i = state_ssm_im[0].reshape(nb, N_STATES)
    x2_s, sr_s, si_s = _s5(xs1, x0r, x0i, *s5w, chain=False, steps=dec)

    wg1, wu1, wd1 = (moe_w_gate[1].astype(BF16), moe_w_up[1].astype(BF16),
                     moe_w_down[1].astype(BF16))
    y_p = _moe(x2_p, nf1, wr1, br1, wg1, wu1, wd1, nfin, True, tm=1024, in_rows=True)
    y_s = _moe(x2_s, nf1, wr1, br1, wg1, wu1, wd1, nfin, True, tm=512, in_rows=True)

    hshape = (N_HEADS, V_DIM)
    gshape = (N_SSM_GROUPS, SSM_STATE)
    return (y_p.reshape(1, seq, D_MODEL), y_s.reshape(nb, dec, D_MODEL),
            k_p.reshape(1, 1, seq, *hshape), v_p.reshape(1, 1, seq, *hshape),
            sr_p[SUBLANES - 1].reshape(1, 1, *gshape), si_p[SUBLANES - 1].reshape(1, 1, *gshape),
            k_s.reshape(1, nb, dec, *hshape), v_s.reshape(1, nb, dec, *hshape),
            sr_s.reshape(1, nb, *gshape), si_s.reshape(1, nb, *gshape))
```
